```python
import math
import jax, jax.numpy as jnp
from jax import lax
import numpy as np

D_MODEL = 2048
BATCH = 2
SEQ = 16384
DEPTH = 1

PLE_DIM = 256
D_MIX = D_MODEL
MLSTM_HEADS = 4
MLSTM_HEAD_DIM = (D_MIX // 2) // MLSTM_HEADS
MLSTM_WIDTH = MLSTM_HEADS * MLSTM_HEAD_DIM
MLSTM_CHUNK = 64
FOX_HEADS = 8
FOX_HEAD_DIM = (D_MIX - MLSTM_WIDTH) // FOX_HEADS
FOX_WIDTH = FOX_HEADS * FOX_HEAD_DIM
FOX_BLOCK = 128
D_FF = ((8 * D_MODEL // 3 + 255) // 256) * 256
CONV_WIDTH = 3
EPS = 1e-6
IN_SIZES = (MLSTM_WIDTH, MLSTM_WIDTH, MLSTM_WIDTH, MLSTM_WIDTH, MLSTM_HEADS, MLSTM_HEADS,
            FOX_WIDTH, FOX_WIDTH, FOX_WIDTH, FOX_HEADS)
IN_TOTAL = sum(IN_SIZES)

kernel_name = "hymba_style_mlstm_fox_convffn_ple"


def rms_norm(x, g):
    xf = x.astype(jnp.float32)
    y = xf * lax.rsqrt(jnp.mean(xf * xf, axis=-1, keepdims=True) + EPS)
    return (y * g.astype(jnp.float32)).astype(x.dtype)


def mlstm_chunkwise(q, k, v, ig, lf):
    B, H, S, d = q.shape
    L = MLSTM_CHUNK
    nc = S // L
    k = k * (d ** -0.5)

    def to_chunks(t):
        return jnp.moveaxis(t.reshape((B, H, nc, L) + t.shape[3:]), 2, 0)

    qc, kc, vc = to_chunks(q), to_chunks(k), to_chunks(v)
    igc = to_chunks(ig)
    bc = jnp.cumsum(to_chunks(lf), axis=-1)
    causal = jnp.tril(jnp.ones((L, L), dtype=bool))

    def step(carry, inp):
        C, n, m = carry
        qi, ki, vi, ii, bi = inp
        inter = bi + m[..., None]
        D = jnp.where(causal, bi[..., :, None] - bi[..., None, :] + ii[..., None, :], -jnp.inf)
        m_comb = jnp.maximum(inter, jnp.max(D, axis=-1))
        w_intra = jnp.exp(D - m_comb[..., None])
        w_inter = jnp.exp(inter - m_comb)
        s = jnp.einsum('bhtd,bhsd->bhts', qi, ki) * w_intra
        num = jnp.einsum('bhts,bhsv->bhtv', s, vi) + w_inter[..., None] * jnp.einsum('bhtk,bhkv->bhtv', qi, C)
        den = jnp.sum(s, axis=-1) + w_inter * jnp.einsum('bhtk,bhk->bht', qi, n)
        h = num / jnp.maximum(jnp.abs(den), jnp.exp(-m_comb))[..., None]
        bL = bi[..., -1]
        g = bL[..., None] - bi + ii
        m_new = jnp.maximum(bL + m, jnp.max(g, axis=-1))
        wk = jnp.exp(g - m_new[..., None])
        decay = jnp.exp(bL + m - m_new)
        C = decay[..., None, None] * C + jnp.einsum('bhsk,bhsv->bhkv', ki * wk[..., None], vi)
        n = decay[..., None] * n + jnp.einsum('bhs,bhsk->bhk', wk, ki)
        return (C, n, m_new), h

    init = (jnp.zeros((B, H, d, d), jnp.float32), jnp.zeros((B, H, d), jnp.float32),
            jnp.zeros((B, H), jnp.float32))
    _, hs = lax.scan(step, init, (qc, kc, vc, igc, bc))
    return jnp.moveaxis(hs, 0, 2).reshape(B, H, S, d)


def forgetting_attention(q, k, v, logf):
    B, S, H, d = q.shape
    nb = S // FOX_BLOCK
    F = jnp.cumsum(logf, axis=1).transpose(0, 2, 1)
    kt = k.transpose(0, 2, 1, 3)
    vt = v.transpose(0, 2, 1, 3)
    qb = q.reshape(B, nb, FOX_BLOCK, H, d).transpose(1, 0, 3, 2, 4)
    Fq = F.reshape(B, H, nb, FOX_BLOCK).transpose(2, 0, 1, 3)
    kpos = jnp.arange(S)
    scale = d ** -0.5

    def one_block(args):
        qi, Fqi, bi = args
        qpos = bi * FOX_BLOCK + jnp.arange(FOX_BLOCK)
        s = jnp.einsum('bhqd,bhkd->bhqk', qi, kt, preferred_element_type=jnp.float32) * scale
        s = s + Fqi[..., None] - F[:, :, None, :]
        s = jnp.where(kpos[None, :] <= qpos[:, None], s, -jnp.inf)
        pr = jax.nn.softmax(s, axis=-1)
        return jnp.einsum('bhqk,bhkd->bhqd', pr.astype(vt.dtype), vt)

    out = lax.map(one_block, (qb, Fq, jnp.arange(nb)))
    return out.transpose(1, 0, 3, 2, 4).reshape(B, S, H * d)


def causal_dwconv(a, w, b):
    y = lax.conv_general_dilated(a, w[:, None, :].astype(a.dtype), window_strides=(1,),
                                 padding=[(CONV_WIDTH - 1, 0)],
                                 dimension_numbers=('NWC', 'WIO', 'NWC'),
                                 feature_group_count=a.shape[-1])
    return y + b.astype(a.dtype)


def setup_inputs(seed: int = 0) -> dict:
    key = jax.random.key(seed)
    ks = jax.random.split(key, 20)
    nrm = jax.random.normal
    f32 = jnp.float32
    x = nrm(ks[0], (BATCH, SEQ, D_MODEL), f32)
    p = nrm(ks[1], (DEPTH, BATCH, SEQ, PLE_DIM), f32)
    mix_norm_g = 1.0 + 0.02 * nrm(ks[2], (DEPTH, D_MODEL), f32)
    w_in = nrm(ks[3], (DEPTH, D_MODEL, IN_TOTAL), f32) * D_MODEL ** -0.5
    m_i_bias = 0.1 * nrm(ks[4], (DEPTH, MLSTM_HEADS), f32)
    m_f_bias = jnp.linspace(3.0, 6.0, MLSTM_HEADS, dtype=f32)[None, :] + 0.01 * nrm(ks[5], (DEPTH, MLSTM_HEADS), f32)
    mlstm_gate_bias = jnp.concatenate([m_i_bias, m_f_bias], axis=-1)
    mlstm_out_g = 1.0 + 0.02 * nrm(ks[6], (DEPTH, MLSTM_WIDTH), f32)
    fox_q_g = 1.0 + 0.02 * nrm(ks[7], (DEPTH, FOX_HEAD_DIM), f32)
    fox_k_g = 1.0 + 0.02 * nrm(ks[8], (DEPTH, FOX_HEAD_DIM), f32)
    fox_f_bias = jnp.linspace(1.0, 5.0, FOX_HEADS, dtype=f32)[None, :] + 0.01 * nrm(ks[9], (DEPTH, FOX_HEADS), f32)
    w_out = nrm(ks[10], (DEPTH, D_MIX, D_MODEL), f32) * D_MIX ** -0.5
    ffn_norm_g = 1.0 + 0.02 * nrm(ks[11], (DEPTH, D_MODEL), f32)
    w_up = nrm(ks[12], (DEPTH, D_MODEL, 2 * D_FF), f32) * D_MODEL ** -0.5
    conv_w = nrm(ks[13], (DEPTH, CONV_WIDTH, D_FF), f32) * CONV_WIDTH ** -0.5
    conv_b = 0.02 * nrm(ks[14], (DEPTH, D_FF), f32)
    w_down = nrm(ks[15], (DEPTH, D_FF, D_MODEL), f32) * D_FF ** -0.5
    ple_norm_g = 1.0 + 0.02 * nrm(ks[16], (DEPTH, D_MODEL), f32)
    w_ple_gate = nrm(ks[17], (DEPTH, D_MODEL, D_MODEL), f32) * D_MODEL ** -0.5
    w_ple_proj = nrm(ks[18], (DEPTH, PLE_DIM, D_MODEL), f32) * PLE_DIM ** -0.5
    return {"x": x, "p": p, "mix_norm_g": mix_norm_g, "w_in": w_in,
            "mlstm_gate_bias": mlstm_gate_bias, "mlstm_out_g": mlstm_out_g,
            "fox_q_g": fox_q_g, "fox_k_g": fox_k_g, "fox_f_bias": fox_f_bias,
            "w_out": w_out, "ffn_norm_g": ffn_norm_g, "w_up": w_up, "conv_w": conv_w,
            "conv_b": conv_b, "w_down": w_down, "ple_norm_g": ple_norm_g,
            "w_ple_gate": w_ple_gate, "w_ple_proj": w_ple_proj}


def reference(x, p, mix_norm_g, w_in, mlstm_gate_bias, mlstm_out_g, fox_q_g, fox_k_g, fox_f_bias,
              w_out, ffn_norm_g, w_up, conv_w, conv_b, w_down, ple_norm_g, w_ple_gate, w_ple_proj):
    B, S, _ = x.shape
    f32 = jnp.float32
    split_points = np.cumsum(IN_SIZES)[:-1].tolist()
    for i in range(DEPTH):
        h = rms_norm(x, mix_norm_g[i])
        z = h @ w_in[i]
        mq, mk, mv, mo, mi, mf, fq, fk, fv, ff = jnp.split(z, split_points, axis=-1)

        def heads_a(t):
            return t.reshape(B, S, MLSTM_HEADS, MLSTM_HEAD_DIM).transpose(0, 2, 1, 3).astype(f32)
        ig = (mi + mlstm_gate_bias[i, :MLSTM_HEADS]).astype(f32).transpose(0, 2, 1)
        lf = jax.nn.log_sigmoid((mf + mlstm_gate_bias[i, MLSTM_HEADS:]).astype(f32)).transpose(0, 2, 1)
        ha = mlstm_chunkwise(heads_a(mq), heads_a(mk), heads_a(mv), ig, lf)
        ha = rms_norm(ha.transpose(0, 2, 1, 3), mlstm_out_g[i].reshape(MLSTM_HEADS, MLSTM_HEAD_DIM))
        ha = (ha.reshape(B, S, MLSTM_WIDTH) * jax.nn.sigmoid(mo.astype(f32))).astype(x.dtype)

        qb = rms_norm(fq.reshape(B, S, FOX_HEADS, FOX_HEAD_DIM), fox_q_g[i])
        kb = rms_norm(fk.reshape(B, S, FOX_HEADS, FOX_HEAD_DIM), fox_k_g[i])
        vb = fv.reshape(B, S, FOX_HEADS, FOX_HEAD_DIM)
        logf = jax.nn.log_sigmoid((ff + fox_f_bias[i]).astype(f32))
        hb = forgetting_attention(qb, kb, vb, logf).astype(x.dtype)

        x = x + jnp.concatenate([ha, hb], axis=-1) @ w_out[i]

        h = rms_norm(x, ffn_norm_g[i])
        u = h @ w_up[i]
        a, g = jnp.split(u, 2, axis=-1)
        a = causal_dwconv(a, conv_w[i], conv_b[i])
        x = x + (jax.nn.gelu(a, approximate=False) * g) @ w_down[i]

        gate = jax.nn.sigmoid(rms_norm(x, ple_norm_g[i]) @ w_ple_gate[i])
        x = x + gate * (p[i].astype(x.dtype) @ w_ple_proj[i])
    return x
```

```python
import functools

import jax
import jax.numpy as jnp
from jax import lax
from jax.experimental import pallas as pl
from jax.experimental.pallas import tpu as pltpu

F32 = jnp.float32
BF16 = jnp.bfloat16
EPS = 1e-6
LANES = 128
BF16_SUBLANES = 16
NEG_BIG = -1e30
MIB = 1024 * 1024

TILES = dict(
    in_tm=1024, mlstm_chunk=256, gate_blk=512, fox_tq=512, fox_tk=512,
    out_tm=512, up_tm=1024, up_tn=512, down_tm=512, down_tn=1024, ple_tm=512,
)


def _fit(n, pref, align):
    t = min(n, pref)
    while t > align and n % t:
        t //= 2
    assert n % t == 0 and (t % align == 0 or t == n), (n, pref, align)
    return t


def _params(semantics, vmem_mib):
    return pltpu.CompilerParams(dimension_semantics=semantics, vmem_limit_bytes=vmem_mib * MIB)


def _rms(x, g):
    return x * lax.rsqrt(jnp.mean(x * x, axis=-1, keepdims=True) + EPS) * g


def _dot(a, b):
    return jnp.dot(a, b, preferred_element_type=F32)


def _dot_nt(a, b):
    return lax.dot_general(a, b, (((1,), (1,)), ((), ())), preferred_element_type=F32)


def _dot_tn(a, b):
    return lax.dot_general(a, b, (((0,), (0,)), ((), ())), preferred_element_type=F32)


def _log_sigmoid(v):
    return jnp.minimum(v, 0.0) - jnp.log1p(jnp.exp(-jnp.abs(v)))


def _in_proj_kernel(x_ref, g_ref, wg_ref, w_ref, qg_ref, kg_ref, z_ref, zg_ref, h_scr, *, jq, jk, fox_d):
    j = pl.program_id(1)

    @pl.when(j == 0)
    def _():
        hb = _rms(x_ref[...], g_ref[...]).astype(BF16)
        h_scr[...] = hb
        zg_ref[...] = _dot(hb, wg_ref[...])

    acc = _dot(h_scr[...], w_ref[...])

    def head_norm(gain_ref):
        for hh in range(acc.shape[1] // fox_d):
            sl = slice(hh * fox_d, (hh + 1) * fox_d)
            z_ref[:, sl] = _rms(acc[:, sl], gain_ref[...]).astype(BF16)

    @pl.when(j == jq)
    def _():
        head_norm(qg_ref)

    @pl.when(j == jk)
    def _():
        head_norm(kg_ref)

    @pl.when((j != jq) & (j != jk))
    def _():
        z_ref[...] = acc.astype(BF16)


def _in_proj(x2d, g, w_main, w_gate, qg, kg, *, tn, jq, jk):
    n, d = x2d.shape
    tm = _fit(n, TILES["in_tm"], 8)
    nt = w_main.shape[1] // tn
    fox_d = qg.shape[1]
    kern = functools.partial(_in_proj_kernel, jq=jq, jk=jk, fox_d=fox_d)
    return pl.pallas_call(
        kern,
        grid=(n // tm, nt),
        in_specs=[
            pl.BlockSpec((tm, d), lambda i, j: (i, 0)),
            pl.BlockSpec((1, d), lambda i, j: (0, 0)),
            pl.BlockSpec((d, LANES), lambda i, j: (0, 0)),
            pl.BlockSpec((d, tn), lambda i, j: (0, j)),
            pl.BlockSpec((1, fox_d), lambda i, j: (0, 0)),
            pl.BlockSpec((1, fox_d), lambda i, j: (0, 0)),
        ],
        out_specs=[
            pl.BlockSpec((tm, tn), lambda i, j: (i, j)),
            pl.BlockSpec((tm, LANES), lambda i, j: (i, 0)),
        ],
        out_shape=[jax.ShapeDtypeStruct((n, nt * tn), BF16), jax.ShapeDtypeStruct((n, LANES), F32)],
        scratch_shapes=[pltpu.VMEM((tm, d), BF16)],
        compiler_params=_params(("parallel", "arbitrary"), 56),
        name="in_proj",
    )(x2d, g, w_gate, w_main, qg, kg)


def _gates_kernel(zg_ref, b_ref, col_ref, row_ref, carry_scr, *, n_pass, n_rows):
    @pl.when(pl.program_id(1) == 0)
    def _():
        carry_scr[...] = jnp.zeros_like(carry_scr)

    v = zg_ref[...] + b_ref[...]
    blk = v.shape[0]
    lane = lax.broadcasted_iota(jnp.int32, v.shape, 1)
    summed = lane >= n_pass
    ls = jnp.where(summed, _log_sigmoid(v), 0.0)
    tri = (lax.broadcasted_iota(jnp.int32, (blk, blk), 0) >= lax.broadcasted_iota(jnp.int32, (blk, blk), 1)).astype(BF16)
    hi = ls.astype(BF16)
    r1 = ls - hi.astype(F32)
    mid = r1.astype(BF16)
    lo = (r1 - mid.astype(F32)).astype(BF16)
    cs = _dot(tri, hi) + _dot(tri, mid) + _dot(tri, lo) + carry_scr[0:1, :]
    carry_scr[...] = jnp.broadcast_to(cs[blk - 1:blk, :], carry_scr.shape)
    out = jnp.where(summed, cs, v)
    col_ref[...] = out
    row_ref[...] = out.T[:n_rows, :]


def _gates(zg, bias, *, batch, seq, n_pass, n_rows):
    blk = _fit(seq, TILES["gate_blk"], LANES)
    nb = seq // blk
    kern = functools.partial(_gates_kernel, n_pass=n_pass, n_rows=n_rows)
    return pl.pallas_call(
        kern,
        grid=(batch, nb),
        in_specs=[
            pl.BlockSpec((blk, LANES), lambda b, s: (b * nb + s, 0)),
            pl.BlockSpec((1, LANES), lambda b, s: (0, 0)),
        ],
        out_specs=[
            pl.BlockSpec((blk, LANES), lambda b, s: (b * nb + s, 0)),
            pl.BlockSpec((None, n_rows, blk), lambda b, s: (b, 0, s)),
        ],
        out_shape=[jax.ShapeDtypeStruct((batch * seq, LANES), F32), jax.ShapeDtypeStruct((batch, n_rows, seq), F32)],
        scratch_shapes=[pltpu.VMEM((8, LANES), F32)],
        compiler_params=_params(("arbitrary", "arbitrary"), 32),
        name="gate_prep",
    )(zg, bias)


def _mlstm_kernel(q_ref, k_ref, v_ref, o_ref, row_ref, col_ref, og_ref, out_ref, c_scr, n_scr, m_scr, *, heads, d):
    @pl.when(pl.program_id(1) == 0)
    def _():
        c_scr[...] = jnp.zeros_like(c_scr)
        n_scr[...] = jnp.zeros_like(n_scr)
        m_scr[...] = jnp.zeros_like(m_scr)

    L = q_ref.shape[0]
    causal = lax.broadcasted_iota(jnp.int32, (L, L), 0) >= lax.broadcasted_iota(jnp.int32, (L, L), 1)
    for hh in range(heads):
        sl = slice(hh * d, (hh + 1) * d)
        q, k, v = q_ref[:, sl], k_ref[:, sl], v_ref[:, sl]
        a_row = row_ref[hh:hh + 1, :] - row_ref[heads + hh:heads + hh + 1, :]
        g_col = col_ref[:, heads + hh:heads + hh + 1]
        a_col = col_ref[:, hh:hh + 1] - g_col
        m_old = m_scr[hh, 0:1, 0:1]

        mm = jnp.maximum(m_old, jnp.max(jnp.where(causal, a_row, -jnp.inf), axis=-1, keepdims=True))
        w = jnp.exp(jnp.where(causal, a_row - mm, -jnp.inf))
        s = _dot_nt(q, k) * w
        w_inter = jnp.exp(m_old - mm)
        num = _dot(s.astype(BF16), v) + w_inter * _dot(q, c_scr[hh].astype(BF16))
        qn = jnp.sum(q.astype(F32) * n_scr[hh], axis=-1, keepdims=True)
        den = jnp.sum(s, axis=-1, keepdims=True) + w_inter * qn
        hout = num / jnp.maximum(jnp.abs(den), jnp.exp(-(g_col + mm)))
        gate = jax.nn.sigmoid(o_ref[:, sl].astype(F32))
        out_ref[:, sl] = (_rms(hout, og_ref[:, sl]) * gate).astype(BF16)

        m_new = jnp.maximum(m_old, jnp.max(a_row, axis=-1, keepdims=True))
        decay = jnp.exp(m_old - m_new)
        kw = k.astype(F32) * jnp.exp(a_col - m_new)
        c_scr[hh] = decay * c_scr[hh] + _dot_tn(kw.astype(BF16), v)
        n_scr[hh] = decay * n_scr[hh] + jnp.sum(kw, axis=0, keepdims=True)
        m_scr[hh] = jnp.broadcast_to(m_new, m_scr.shape[1:])


def _mlstm(z, rows, cols, out_g, *, batch, seq, heads, d):
    L = _fit(seq, TILES["mlstm_chunk"], LANES)
    nc = seq // L
    width = heads * d
    n_rows = rows.shape[1]
    kern = functools.partial(_mlstm_kernel, heads=heads, d=d)

    def zspec(group):
        return pl.BlockSpec((L, width), lambda b, c: (b * nc + c, group))

    return pl.pallas_call(
        kern,
        grid=(batch, nc),
        in_specs=[
            zspec(0), zspec(1), zspec(2), zspec(3),
            pl.BlockSpec((None, n_rows, L), lambda b, c: (b, 0, c)),
            pl.BlockSpec((L, LANES), lambda b, c: (b * nc + c, 0)),
            pl.BlockSpec((1, width), lambda b, c: (0, 0)),
        ],
        out_specs=pl.BlockSpec((L, width), lambda b, c: (b * nc + c, 0)),
        out_shape=jax.ShapeDtypeStruct((batch * seq, width), BF16),
        scratch_shapes=[pltpu.VMEM((heads, d, d), F32), pltpu.VMEM((heads, 1, d), F32), pltpu.VMEM((heads, 8, LANES), F32)],
        compiler_params=_params(("parallel", "arbitrary"), 32),
        name="mlstm",
    )(z, z, z, z, rows, cols, out_g)


def _fox_kernel(q_ref, k_ref, v_ref, f_ref, o_ref, *, tq, tk):
    i = pl.program_id(2)
    q = q_ref[...]
    q0 = pl.multiple_of(i * tq, tq)
    f_q0 = f_ref[:, pl.ds(q0, LANES)][:, 0:1]

    def block(k0, carry, masked):
        m, l, acc = carry
        s = _dot_nt(q, k_ref[pl.ds(k0, tk), :]) + (f_q0 - f_ref[:, pl.ds(k0, tk)])
        if masked:
            qpos = q0 + lax.broadcasted_iota(jnp.int32, (tq, tk), 0)
            kpos = k0 + lax.broadcasted_iota(jnp.int32, (tq, tk), 1)
            s = jnp.where(kpos <= qpos, s, NEG_BIG)
        m_new = jnp.maximum(m, jnp.max(s, axis=-1, keepdims=True))
        alpha = jnp.exp(m - m_new)
        p = jnp.exp(s - m_new)
        l = alpha * l + jnp.sum(p, axis=-1, keepdims=True)
        acc = alpha * acc + _dot(p.astype(BF16), v_ref[pl.ds(k0, tk), :])
        return m_new, l, acc

    init = (jnp.full((tq, 1), NEG_BIG, F32), jnp.zeros((tq, 1), F32), jnp.zeros((tq, q.shape[1]), F32))
    carry = lax.fori_loop(0, (i * tq) // tk, lambda j, c: block(pl.multiple_of(j * tk, tk), c, False), init)
    for jj in range(tq // tk):
        carry = block(q0 + jj * tk, carry, True)
    _, l, acc = carry
    o_ref[...] = (acc / l).astype(BF16)


def _fox(z, frow, *, batch, seq, heads, d, q_group, f_row0):
    tq = _fit(seq, TILES["fox_tq"], LANES)
    tk = _fit(tq, TILES["fox_tk"], LANES)
    nq = seq // tq
    per_group = heads
    kern = functools.partial(_fox_kernel, tq=tq, tk=tk)
    f4 = frow.reshape(batch, frow.shape[1], 1, seq)
    return pl.pallas_call(
        kern,
        grid=(batch, heads, nq),
        in_specs=[
            pl.BlockSpec((tq, d), lambda b, h, i: (b * nq + i, q_group * per_group + h)),
            pl.BlockSpec((seq, d), lambda b, h, i: (b, (q_group + 1) * per_group + h)),
            pl.BlockSpec((seq, d), lambda b, h, i: (b, (q_group + 2) * per_group + h)),
            pl.BlockSpec((None, None, 1, seq), lambda b, h, i: (b, f_row0 + h, 0, 0)),
        ],
        out_specs=pl.BlockSpec((tq, d), lambda b, h, i: (b * nq + i, h)),
        out_shape=jax.ShapeDtypeStruct((batch * seq, heads * d), BF16),
        compiler_params=_params(("parallel", "parallel", "arbitrary"), 48),
        name="fox_attention",
    )(z, z, z, f4)


def _out_proj_kernel(x_ref, ha_ref, hb_ref, wa_ref, wb_ref, g_ref, x1_ref, h_ref):
    x1 = x_ref[...] + _dot(ha_ref[...], wa_ref[...]) + _dot(hb_ref[...], wb_ref[...])
    x1_ref[...] = x1
    h_ref[...] = _rms(x1, g_ref[...]).astype(BF16)


def _out_proj(x2d, ha, hb, wa, wb, g):
    n, d = x2d.shape
    tm = _fit(n, TILES["out_tm"], BF16_SUBLANES)
    ka, kb = ha.shape[1], hb.shape[1]
    const = dict(pipeline_mode=pl.Buffered(1))
    return pl.pallas_call(
        _out_proj_kernel,
        grid=(n // tm,),
        in_specs=[
            pl.BlockSpec((tm, d), lambda i: (i, 0)),
            pl.BlockSpec((tm, ka), lambda i: (i, 0)),
            pl.BlockSpec((tm, kb), lambda i: (i, 0)),
            pl.BlockSpec((ka, d), lambda i: (0, 0), **const),
            pl.BlockSpec((kb, d), lambda i: (0, 0), **const),
            pl.BlockSpec((1, d), lambda i: (0, 0)),
        ],
        out_specs=[pl.BlockSpec((tm, d), lambda i: (i, 0)), pl.BlockSpec((tm, d), lambda i: (i, 0))],
        out_shape=[jax.ShapeDtypeStruct((n, d), F32), jax.ShapeDtypeStruct((n, d), BF16)],
        compiler_params=_params(("parallel",), 56),
        name="out_proj",
    )(x2d, ha, hb, wa, wb, g)


def _ffn_up_kernel(h_ref, halo_ref, wa_ref, wg_ref, cw_ref, cb_ref, act_ref, a_scr, *, tiles_per_seq, halo):
    i = pl.program_id(0)
    tm = h_ref.shape[0]
    h = h_ref[...]
    a = _dot(h, wa_ref[...])
    a_prev = _dot(halo_ref[...], wa_ref[...])
    a_prev = jnp.where(i % tiles_per_seq == 0, 0.0, a_prev)
    a_scr[0:halo, :] = a_prev
    a_scr[halo:, :] = a
    y = (cw_ref[2:3, :] * a + cw_ref[1:2, :] * a_scr[pl.ds(halo - 1, tm), :]
         + cw_ref[0:1, :] * a_scr[pl.ds(halo - 2, tm), :] + cb_ref[...])
    gelu = 0.5 * y * (1.0 + lax.erf(y * (0.5 ** 0.5)))
    act_ref[...] = (gelu * _dot(h, wg_ref[...])).astype(BF16)


def _ffn_up(h2, w_up, conv_w, conv_b, *, seq):
    n, d = h2.shape
    d_ff = conv_w.shape[1]
    halo = BF16_SUBLANES
    tm = _fit(seq, TILES["up_tm"], halo)
    tn = _fit(d_ff, TILES["up_tn"], LANES)
    nj = d_ff // tn
    kern = functools.partial(_ffn_up_kernel, tiles_per_seq=seq // tm, halo=halo)
    return pl.pallas_call(
        kern,
        grid=(n // tm, nj),
        in_specs=[
            pl.BlockSpec((tm, d), lambda i, j: (i, 0)),
            pl.BlockSpec((halo, d), lambda i, j: (jnp.maximum(i * (tm // halo) - 1, 0), 0)),
            pl.BlockSpec((d, tn), lambda i, j: (0, j)),
            pl.BlockSpec((d, tn), lambda i, j: (0, nj + j)),
            pl.BlockSpec((conv_w.shape[0], tn), lambda i, j: (0, j)),
            pl.BlockSpec((1, tn), lambda i, j: (0, j)),
        ],
        out_specs=pl.BlockSpec((tm, tn), lambda i, j: (i, j)),
        out_shape=jax.ShapeDtypeStruct((n, d_ff), BF16),
        scratch_shapes=[pltpu.VMEM((tm + halo, tn), F32)],
        compiler_params=_params(("parallel", "arbitrary"), 48),
        name="ffn_up",
    )(h2, h2, w_up, w_up, conv_w, conv_b)


def _ffn_down_kernel(act_ref, w_ref, x_ref, o_ref):
    o_ref[...] = x_ref[...] + _dot(act_ref[...], w_ref[...])


def _ffn_down(act, w_down, x1):
    n, d_ff = act.shape
    d = w_down.shape[1]
    tm = _fit(n, TILES["down_tm"], BF16_SUBLANES)
    tn = _fit(d, TILES["down_tn"], LANES)
    return pl.pallas_call(
        _ffn_down_kernel,
        grid=(d // tn, n // tm),
        in_specs=[
            pl.BlockSpec((tm, d_ff), lambda j, i: (i, 0)),
            pl.BlockSpec((d_ff, tn), lambda j, i: (0, j)),
            pl.BlockSpec((tm, tn), lambda j, i: (i, j)),
        ],
        out_specs=pl.BlockSpec((tm, tn), lambda j, i: (i, j)),
        out_shape=jax.ShapeDtypeStruct((n, d), F32),
        compiler_params=_params(("arbitrary", "arbitrary"), 56),
        name="ffn_down",
    )(act, w_down, x1)


def _ple_kernel(x_ref, p_ref, g_ref, wg_ref, wp_ref, o_ref):
    x = x_ref[...]
    gate = jax.nn.sigmoid(_dot(_rms(x, g_ref[...]).astype(BF16), wg_ref[...]))
    o_ref[...] = x + gate * _dot(p_ref[...].astype(BF16), wp_ref[...])


def _ple(x2, p2d, g, w_gate, w_proj):
    n, d = x2.shape
    dp = p2d.shape[1]
    tm = _fit(n, TILES["ple_tm"], 8)
    const = dict(pipeline_mode=pl.Buffered(1))
    return pl.pallas_call(
        _ple_kernel,
        grid=(n // tm,),
        in_specs=[
            pl.BlockSpec((tm, d), lambda i: (i, 0)),
            pl.BlockSpec((tm, dp), lambda i: (i, 0)),
            pl.BlockSpec((1, d), lambda i: (0, 0)),
            pl.BlockSpec((d, d), lambda i: (0, 0), **const),
            pl.BlockSpec((dp, d), lambda i: (0, 0), **const),
        ],
        out_specs=pl.BlockSpec((tm, d), lambda i: (i, 0)),
        out_shape=jax.ShapeDtypeStruct((n, d), F32),
        compiler_params=_params(("parallel",), 56),
        name="ple",
    )(x2, p2d, g, w_gate, w_proj)


def kernel(x, p, mix_norm_g, w_in, mlstm_gate_bias, mlstm_out_g, fox_q_g, fox_k_g, fox_f_bias, w_out, ffn_norm_g, w_up, conv_w, conv_b, w_down, ple_norm_g, w_ple_gate, w_ple_proj):
    batch, seq, d_model = x.shape
    depth = w_in.shape[0]
    mh = mlstm_gate_bias.shape[1] // 2
    mw = mlstm_out_g.shape[1]
    md = mw // mh
    fh = fox_f_bias.shape[1]
    fd = fox_q_g.shape[1]
    fw = fh * fd
    assert mw == fw and 2 * mh + fh <= LANES and w_in.shape[2] == 4 * mw + 2 * mh + 3 * fw + fh
    n_gate = 2 * mh + fh

    xs = x.reshape(batch * seq, d_model)
    for li in range(depth):
        w = w_in[li]
        o_mi = 4 * mw
        o_fq = o_mi + 2 * mh
        o_ff = o_fq + 3 * fw
        w_main = jnp.concatenate([w[:, :mw], w[:, mw:2 * mw] * (md ** -0.5), w[:, 2 * mw:o_mi], w[:, o_fq:o_ff]], axis=1).astype(BF16)
        w_gate = jnp.concatenate([w[:, o_mi:o_fq], w[:, o_ff:], jnp.zeros((d_model, LANES - n_gate), F32)], axis=1).astype(BF16)
        gate_bias = jnp.concatenate([mlstm_gate_bias[li], fox_f_bias[li], jnp.zeros((LANES - n_gate,), F32)])[None, :]
        q_gain = (fox_q_g[li] * (fd ** -0.5))[None, :]
        k_gain = fox_k_g[li][None, :]

        z, zg = _in_proj(xs, mix_norm_g[li][None, :], w_main, w_gate, q_gain, k_gain, tn=mw, jq=4, jk=5)
        cols, rows = _gates(zg, gate_bias, batch=batch, seq=seq, n_pass=mh, n_rows=-(-n_gate // 8) * 8)
        ha = _mlstm(z, rows, cols, mlstm_out_g[li][None, :], batch=batch, seq=seq, heads=mh, d=md)
        hb = _fox(z, rows, batch=batch, seq=seq, heads=fh, d=fd, q_group=4, f_row0=2 * mh)

        wo = w_out[li].astype(BF16)
        x1, h2 = _out_proj(xs, ha, hb, wo[:mw], wo[mw:], ffn_norm_g[li][None, :])
        act = _ffn_up(h2, w_up[li].astype(BF16), conv_w[li], conv_b[li][None, :], seq=seq)
        x2 = _ffn_down(act, w_down[li].astype(BF16), x1)
        xs = _ple(x2, p[li].reshape(batch * seq, -1), ple_norm_g[li][None, :], w_ple_gate[li].astype(BF16), w_ple_proj[li].astype(BF16))
    return xs.reshape(batch, seq, d_model)
```

```python
import functools
import math

import jax
import jax.numpy as jnp
from jax import lax
from jax.experimental import pallas as pl
from jax.experimental.pallas import tpu as pltpu

F32 = jnp.float32
BF16 = jnp.bfloat16
EPS = 1e-6
LANES = 128
BF16_SUBLANES = 16
NEG_BIG = -1e30
LOG2E = math.log2(math.e)
N_BIAS_TERMS = 3
MIB = 1024 * 1024

TILES = dict(
    in_tm=1024, mlstm_chunk=256, fox_tq=1024, fox_tk=512,
    out_tm=512, up_tm=1024, up_tn=512, down_tm=512, down_tn=1024, ple_tm=512,
)


def _fit(n, pref, align):
    t = min(n, pref)
    while t > align and n % t:
        t //= 2
    assert n % t == 0 and (t % align == 0 or t == n), (n, pref, align)
    return t


def _params(semantics, vmem_mib):
    return pltpu.CompilerParams(dimension_semantics=semantics, vmem_limit_bytes=vmem_mib * MIB)


def _rms(x, g):
    return x * lax.rsqrt(jnp.mean(x * x, axis=-1, keepdims=True) + EPS) * g


def _dot(a, b):
    return jnp.dot(a, b, preferred_element_type=F32)


def _dot_nt(a, b):
    return lax.dot_general(a, b, (((1,), (1,)), ((), ())), preferred_element_type=F32)


def _dot_tn(a, b):
    return lax.dot_general(a, b, (((0,), (0,)), ((), ())), preferred_element_type=F32)


def _log_sigmoid(v):
    return jnp.minimum(v, 0.0) - jnp.log1p(jnp.exp(-jnp.abs(v)))


def _in_proj_kernel(x_ref, g_ref, wg_ref, w_ref, qg_ref, kg_ref, z_ref, zg_ref, vt_ref, h_scr, *, jq, jk, jv, fox_d):
    j = pl.program_id(1)

    @pl.when(j == 0)
    def _():
        hb = _rms(x_ref[...], g_ref[...]).astype(BF16)
        h_scr[...] = hb
        zg_ref[...] = _dot(hb, wg_ref[...])

    acc = _dot(h_scr[...], w_ref[...])

    def head_norm(gain_ref):
        for hh in range(acc.shape[1] // fox_d):
            sl = slice(hh * fox_d, (hh + 1) * fox_d)
            z_ref[:, sl] = _rms(acc[:, sl], gain_ref[...]).astype(BF16)

    @pl.when(j == jq)
    def _():
        head_norm(qg_ref)

    @pl.when(j == jk)
    def _():
        head_norm(kg_ref)

    @pl.when(j == jv)
    def _():
        vt_ref[...] = acc.T.astype(BF16)

    @pl.when((j != jq) & (j != jk) & (j != jv))
    def _():
        z_ref[...] = acc.astype(BF16)


def _in_proj(x2d, g, w_main, w_gate, qg, kg, *, batch, tn, jq, jk, jv):
    n, d = x2d.shape
    seq = n // batch
    tm = _fit(seq, TILES["in_tm"], LANES)
    tiles_per_seq = seq // tm
    nt = w_main.shape[1] // tn
    assert jv == nt - 1
    fox_d = qg.shape[1]
    kern = functools.partial(_in_proj_kernel, jq=jq, jk=jk, jv=jv, fox_d=fox_d)
    return pl.pallas_call(
        kern,
        grid=(n // tm, nt),
        in_specs=[
            pl.BlockSpec((tm, d), lambda i, j: (i, 0)),
            pl.BlockSpec((1, d), lambda i, j: (0, 0)),
            pl.BlockSpec((d, LANES), lambda i, j: (0, 0)),
            pl.BlockSpec((d, tn), lambda i, j: (0, j)),
            pl.BlockSpec((1, fox_d), lambda i, j: (0, 0)),
            pl.BlockSpec((1, fox_d), lambda i, j: (0, 0)),
        ],
        out_specs=[
            pl.BlockSpec((tm, tn), lambda i, j: (i, jnp.minimum(j, jv - 1))),
            pl.BlockSpec((tm, LANES), lambda i, j: (i, 0)),
            pl.BlockSpec((None, tn, tm), lambda i, j: (i // tiles_per_seq, 0, i % tiles_per_seq)),
        ],
        out_shape=[
            jax.ShapeDtypeStruct((n, jv * tn), BF16),
            jax.ShapeDtypeStruct((n, LANES), F32),
            jax.ShapeDtypeStruct((batch, tn, seq), BF16),
        ],
        scratch_shapes=[pltpu.VMEM((tm, d), BF16)],
        compiler_params=_params(("parallel", "arbitrary"), 56),
        name="in_proj",
    )(x2d, g, w_gate, w_main, qg, kg)


def _gates_kernel(zg_ref, b_ref, col_ref, row_ref, kb_ref, carry_scr, *, n_pass, n_rows, n_attn):
    @pl.when(pl.program_id(1) == 0)
    def _():
        carry_scr[...] = jnp.zeros_like(carry_scr)

    v = zg_ref[...] + b_ref[...]
    blk = v.shape[0]
    lane = lax.broadcasted_iota(jnp.int32, v.shape, 1)
    summed = lane >= n_pass
    ls = jnp.where(summed, _log_sigmoid(v), 0.0)

    def split3(t):
        hi = t.astype(BF16)
        r1 = t - hi.astype(F32)
        mid = r1.astype(BF16)
        return hi, mid, (r1 - mid.astype(F32)).astype(BF16)

    tri = (lax.broadcasted_iota(jnp.int32, (blk, blk), 0) >= lax.broadcasted_iota(jnp.int32, (blk, blk), 1)).astype(BF16)
    local = sum(_dot(tri, t) for t in split3(ls))
    cs = local + carry_scr[0:1, :]
    carry_scr[...] = jnp.broadcast_to(cs[blk - 1:blk, :], carry_scr.shape)
    out = jnp.where(summed, cs, v)
    col_ref[...] = out
    row_ref[...] = out.T[:n_rows, :]

    a0 = n_rows - n_attn
    src = lax.broadcasted_iota(jnp.int32, (LANES, LANES), 0)
    dst = lax.broadcasted_iota(jnp.int32, (LANES, LANES), 1)
    kb = None
    for c, t in enumerate(split3((local[blk - 1:blk, :] - local) * LOG2E)):
        place = ((src >= a0) & (src < a0 + n_attn) & (dst == N_BIAS_TERMS * (src - a0) + c)).astype(BF16)
        kb = _dot(t, place) if kb is None else kb + _dot(t, place)
    kb_ref[...] = kb.astype(BF16)


def _gates(zg, bias, *, batch, seq, blk, n_pass, n_rows, n_attn):
    nb = seq // blk
    kern = functools.partial(_gates_kernel, n_pass=n_pass, n_rows=n_rows, n_attn=n_attn)
    return pl.pallas_call(
        kern,
        grid=(batch, nb),
        in_specs=[
            pl.BlockSpec((blk, LANES), lambda b, s: (b * nb + s, 0)),
            pl.BlockSpec((1, LANES), lambda b, s: (0, 0)),
        ],
        out_specs=[
            pl.BlockSpec((blk, LANES), lambda b, s: (b * nb + s, 0)),
            pl.BlockSpec((None, n_rows, blk), lambda b, s: (b, 0, s)),
            pl.BlockSpec((blk, LANES), lambda b, s: (b * nb + s, 0)),
        ],
        out_shape=[
            jax.ShapeDtypeStruct((batch * seq, LANES), F32),
            jax.ShapeDtypeStruct((batch, n_rows, seq), F32),
            jax.ShapeDtypeStruct((batch * seq, LANES), BF16),
        ],
        scratch_shapes=[pltpu.VMEM((8, LANES), F32)],
        compiler_params=_params(("arbitrary", "arbitrary"), 32),
        name="gate_prep",
    )(zg, bias)


def _mlstm_kernel(q_ref, k_ref, v_ref, o_ref, row_ref, col_ref, og_ref, out_ref, c_scr, n_scr, m_scr, *, heads, d):
    @pl.when(pl.program_id(1) == 0)
    def _():
        c_scr[...] = jnp.zeros_like(c_scr)
        n_scr[...] = jnp.zeros_like(n_scr)
        m_scr[...] = jnp.zeros_like(m_scr)

    L = q_ref.shape[0]
    causal = lax.broadcasted_iota(jnp.int32, (L, L), 0) >= lax.broadcasted_iota(jnp.int32, (L, L), 1)
    for hh in range(heads):
        sl = slice(hh * d, (hh + 1) * d)
        q, k, v = q_ref[:, sl], k_ref[:, sl], v_ref[:, sl]
        a_row = row_ref[hh:hh + 1, :] - row_ref[heads + hh:heads + hh + 1, :]
        g_col = col_ref[:, heads + hh:heads + hh + 1]
        a_col = col_ref[:, hh:hh + 1] - g_col
        m_old = m_scr[hh, 0:1, 0:1]

        mm = jnp.maximum(m_old, jnp.max(jnp.where(causal, a_row, -jnp.inf), axis=-1, keepdims=True))
        w = jnp.exp(jnp.where(causal, a_row - mm, -jnp.inf))
        s = _dot_nt(q, k) * w
        w_inter = jnp.exp(m_old - mm)
        num = _dot(s.astype(BF16), v) + w_inter * _dot(q, c_scr[hh].astype(BF16))
        qn = jnp.sum(q.astype(F32) * n_scr[hh], axis=-1, keepdims=True)
        den = jnp.sum(s, axis=-1, keepdims=True) + w_inter * qn
        hout = num / jnp.maximum(jnp.abs(den), jnp.exp(-(g_col + mm)))
        gate = jax.nn.sigmoid(o_ref[:, sl].astype(F32))
        out_ref[:, sl] = (_rms(hout, og_ref[:, sl]) * gate).astype(BF16)

        m_new = jnp.maximum(m_old, jnp.max(a_row, axis=-1, keepdims=True))
        decay = jnp.exp(m_old - m_new)
        kw = k.astype(F32) * jnp.exp(a_col - m_new)
        c_scr[hh] = decay * c_scr[hh] + _dot_tn(kw.astype(BF16), v)
        n_scr[hh] = decay * n_scr[hh] + jnp.sum(kw, axis=0, keepdims=True)
        m_scr[hh] = jnp.broadcast_to(m_new, m_scr.shape[1:])


def _mlstm(z, rows, cols, out_g, *, batch, seq, heads, d):
    L = _fit(seq, TILES["mlstm_chunk"], LANES)
    nc = seq // L
    width = heads * d
    n_rows = rows.shape[1]
    kern = functools.partial(_mlstm_kernel, heads=heads, d=d)

    def zspec(group):
        return pl.BlockSpec((L, width), lambda b, c: (b * nc + c, group))

    return pl.pallas_call(
        kern,
        grid=(batch, nc),
        in_specs=[
            zspec(0), zspec(1), zspec(2), zspec(3),
            pl.BlockSpec((None, n_rows, L), lambda b, c: (b, 0, c)),
            pl.BlockSpec((L, LANES), lambda b, c: (b * nc + c, 0)),
            pl.BlockSpec((1, width), lambda b, c: (0, 0)),
        ],
        out_specs=pl.BlockSpec((L, width), lambda b, c: (b * nc + c, 0)),
        out_shape=jax.ShapeDtypeStruct((batch * seq, width), BF16),
        scratch_shapes=[pltpu.VMEM((heads, d, d), F32), pltpu.VMEM((heads, 1, d), F32), pltpu.VMEM((heads, 8, LANES), F32)],
        compiler_params=_params(("parallel", "arbitrary"), 32),
        name="mlstm",
    )(z, z, z, z, rows, cols, out_g)


def _fox_kernel(q_ref, k_ref, kb_ref, vt_ref, f_ref, o_ref, qa_scr, s0_scr, s1_scr, acc_scr, *, tq, tk):
    h = pl.program_id(1)
    i = pl.program_id(2)
    q0 = pl.multiple_of(i * tq, tq)
    lane = lax.broadcasted_iota(jnp.int32, (tq, LANES), 1)
    qa_scr[:, :LANES] = q_ref[...]
    qa_scr[:, LANES:] = jnp.where((lane >= N_BIAS_TERMS * h) & (lane < N_BIAS_TERMS * (h + 1)), 1.0, 0.0).astype(BF16)
    acc_scr[...] = jnp.zeros_like(acc_scr)

    def f_end(k0):
        return f_ref[:, pl.ds(pl.multiple_of(k0 + tk - LANES, LANES), LANES)][:, LANES - 1:LANES]

    f_ref_q = f_end(q0 + tq - tk)

    def scores(k0, s_ref):
        k_aug = jnp.concatenate([k_ref[pl.ds(k0, tk), :], kb_ref[pl.ds(k0, tk), :]], axis=1)
        s_ref[...] = _dot_nt(k_aug, qa_scr[...])

    def update(k0, s_ref, carry, masked):
        m, l = carry
        off = (f_ref_q - f_end(k0)) * LOG2E
        s = s_ref[...]
        if masked:
            kpos = k0 + lax.broadcasted_iota(jnp.int32, (tk, tq), 0)
            qpos = q0 + lax.broadcasted_iota(jnp.int32, (tk, tq), 1)
            s = jnp.where(kpos <= qpos, s, NEG_BIG)
        m_new = jnp.maximum(m, jnp.max(s, axis=0, keepdims=True) + off)
        alpha = jnp.exp2(m - m_new)
        p = jnp.exp2(s - (m_new - off))
        l = alpha * l + jnp.sum(p, axis=0, keepdims=True)
        acc_scr[...] = alpha * acc_scr[...] + _dot(vt_ref[:, pl.ds(k0, tk)], p.astype(BF16))
        return m_new, l

    slots = (s0_scr, s1_scr)
    scores(0, s0_scr)

    def pair(t, carry):
        k0 = pl.multiple_of(t * (2 * tk), 2 * tk)
        scores(k0 + tk, s1_scr)
        carry = update(k0, s0_scr, carry, False)
        scores(k0 + 2 * tk, s0_scr)
        return update(k0 + tk, s1_scr, carry, False)

    carry = (jnp.full((1, tq), NEG_BIG, F32), jnp.zeros((1, tq), F32))
    carry = lax.fori_loop(0, q0 // (2 * tk), pair, carry)
    n_diag = tq // tk
    for jj in range(n_diag):
        if jj + 1 < n_diag:
            scores(q0 + (jj + 1) * tk, slots[(jj + 1) % 2])
        carry = update(q0 + jj * tk, slots[jj % 2], carry, True)
    _, l = carry
    o_ref[...] = (acc_scr[...] / l).T.astype(BF16)


def _fox(z, kb, vt, frow, *, batch, seq, heads, d, tq, tk, q_group, f_row0):
    assert (tq // tk) % 2 == 0
    nq = seq // tq
    per_group = z.shape[1] // (q_group + 2) // d
    kern = functools.partial(_fox_kernel, tq=tq, tk=tk)
    f4 = frow.reshape(batch, frow.shape[1], 1, seq)
    return pl.pallas_call(
        kern,
        grid=(batch, heads, nq),
        in_specs=[
            pl.BlockSpec((tq, d), lambda b, h, i: (b * nq + i, q_group * per_group + h)),
            pl.BlockSpec((seq, d), lambda b, h, i: (b, (q_group + 1) * per_group + h)),
            pl.BlockSpec((seq, LANES), lambda b, h, i: (b, 0)),
            pl.BlockSpec((None, d, seq), lambda b, h, i: (b, h, 0)),
            pl.BlockSpec((None, None, 1, seq), lambda b, h, i: (b, f_row0 + h, 0, 0)),
        ],
        out_specs=pl.BlockSpec((tq, d), lambda b, h, i: (b * nq + i, h)),
        out_shape=jax.ShapeDtypeStruct((batch * seq, heads * d), BF16),
        scratch_shapes=[pltpu.VMEM((tq, 2 * LANES), BF16), pltpu.VMEM((tk, tq), F32), pltpu.VMEM((tk, tq), F32), pltpu.VMEM((d, tq), F32)],
        compiler_params=_params(("parallel", "parallel", "arbitrary"), 48),
        name="fox_attention",
    )(z, z, kb, vt, f4)


def _out_proj_kernel(x_ref, ha_ref, hb_ref, wa_ref, wb_ref, g_ref, x1_ref, h_ref):
    x1 = x_ref[...] + _dot(ha_ref[...], wa_ref[...]) + _dot(hb_ref[...], wb_ref[...])
    x1_ref[...] = x1
    h_ref[...] = _rms(x1, g_ref[...]).astype(BF16)


def _out_proj(x2d, ha, hb, wa, wb, g):
    n, d = x2d.shape
    tm = _fit(n, TILES["out_tm"], BF16_SUBLANES)
    ka, kb = ha.shape[1], hb.shape[1]
    const = dict(pipeline_mode=pl.Buffered(1))
    return pl.pallas_call(
        _out_proj_kernel,
        grid=(n // tm,),
        in_specs=[
            pl.BlockSpec((tm, d), lambda i: (i, 0)),
            pl.BlockSpec((tm, ka), lambda i: (i, 0)),
            pl.BlockSpec((tm, kb), lambda i: (i, 0)),
            pl.BlockSpec((ka, d), lambda i: (0, 0), **const),
            pl.BlockSpec((kb, d), lambda i: (0, 0), **const),
            pl.BlockSpec((1, d), lambda i: (0, 0)),
        ],
        out_specs=[pl.BlockSpec((tm, d), lambda i: (i, 0)), pl.BlockSpec((tm, d), lambda i: (i, 0))],
        out_shape=[jax.ShapeDtypeStruct((n, d), F32), jax.ShapeDtypeStruct((n, d), BF16)],
        compiler_params=_params(("parallel",), 56),
        name="out_proj",
    )(x2d, ha, hb, wa, wb, g)


def _ffn_up_kernel(h_ref, halo_ref, wa_ref, wg_ref, cw_ref, cb_ref, act_ref, a_scr, *, tiles_per_seq, halo):
    i = pl.program_id(0)
    tm = h_ref.shape[0]
    h = h_ref[...]
    a = _dot(h, wa_ref[...])
    a_prev = _dot(halo_ref[...], wa_ref[...])
    a_prev = jnp.where(i % tiles_per_seq == 0, 0.0, a_prev)
    a_scr[0:halo, :] = a_prev
    a_scr[halo:, :] = a
    y = (cw_ref[2:3, :] * a + cw_ref[1:2, :] * a_scr[pl.ds(halo - 1, tm), :]
         + cw_ref[0:1, :] * a_scr[pl.ds(halo - 2, tm), :] + cb_ref[...])
    gelu = 0.5 * y * (1.0 + lax.erf(y * (0.5 ** 0.5)))
    act_ref[...] = (gelu * _dot(h, wg_ref[...])).astype(BF16)


def _ffn_up(h2, w_up, conv_w, conv_b, *, seq):
    n, d = h2.shape
    d_ff = conv_w.shape[1]
    halo = BF16_SUBLANES
    tm = _fit(seq, TILES["up_tm"], halo)
    tn = _fit(d_ff, TILES["up_tn"], LANES)
    nj = d_ff // tn
    kern = functools.partial(_ffn_up_kernel, tiles_per_seq=seq // tm, halo=halo)
    return pl.pallas_call(
        kern,
        grid=(n // tm, nj),
        in_specs=[
            pl.BlockSpec((tm, d), lambda i, j: (i, 0)),
            pl.BlockSpec((halo, d), lambda i, j: (jnp.maximum(i * (tm // halo) - 1, 0), 0)),
            pl.BlockSpec((d, tn), lambda i, j: (0, j)),
            pl.BlockSpec((d, tn), lambda i, j: (0, nj + j)),
            pl.BlockSpec((conv_w.shape[0], tn), lambda i, j: (0, j)),
            pl.BlockSpec((1, tn), lambda i, j: (0, j)),
        ],
        out_specs=pl.BlockSpec((tm, tn), lambda i, j: (i, j)),
        out_shape=jax.ShapeDtypeStruct((n, d_ff), BF16),
        scratch_shapes=[pltpu.VMEM((tm + halo, tn), F32)],
        compiler_params=_params(("parallel", "arbitrary"), 48),
        name="ffn_up",
    )(h2, h2, w_up, w_up, conv_w, conv_b)


def _ffn_down_kernel(act_ref, w_ref, x_ref, o_ref):
    o_ref[...] = x_ref[...] + _dot(act_ref[...], w_ref[...])


def _ffn_down(act, w_down, x1):
    n, d_ff = act.shape
    d = w_down.shape[1]
    tm = _fit(n, TILES["down_tm"], BF16_SUBLANES)
    tn = _fit(d, TILES["down_tn"], LANES)
    return pl.pallas_call(
        _ffn_down_kernel,
        grid=(d // tn, n // tm),
        in_specs=[
            pl.BlockSpec((tm, d_ff), lambda j, i: (i, 0)),
            pl.BlockSpec((d_ff, tn), lambda j, i: (0, j)),
            pl.BlockSpec((tm, tn), lambda j, i: (i, j)),
        ],
        out_specs=pl.BlockSpec((tm, tn), lambda j, i: (i, j)),
        out_shape=jax.ShapeDtypeStruct((n, d), F32),
        compiler_params=_params(("arbitrary", "arbitrary"), 56),
        name="ffn_down",
    )(act, w_down, x1)


def _ple_kernel(x_ref, p_ref, g_ref, wg_ref, wp_ref, o_ref):
    x = x_ref[...]
    gate = jax.nn.sigmoid(_dot(_rms(x, g_ref[...]).astype(BF16), wg_ref[...]))
    o_ref[...] = x + gate * _dot(p_ref[...].astype(BF16), wp_ref[...])


def _ple(x2, p2d, g, w_gate, w_proj):
    n, d = x2.shape
    dp = p2d.shape[1]
    tm = _fit(n, TILES["ple_tm"], 8)
    const = dict(pipeline_mode=pl.Buffered(1))
    return pl.pallas_call(
        _ple_kernel,
        grid=(n // tm,),
        in_specs=[
            pl.BlockSpec((tm, d), lambda i: (i, 0)),
            pl.BlockSpec((tm, dp), lambda i: (i, 0)),
            pl.BlockSpec((1, d), lambda i: (0, 0)),
            pl.BlockSpec((d, d), lambda i: (0, 0), **const),
            pl.BlockSpec((dp, d), lambda i: (0, 0), **const),
        ],
        out_specs=pl.BlockSpec((tm, d), lambda i: (i, 0)),
        out_shape=jax.ShapeDtypeStruct((n, d), F32),
        compiler_params=_params(("parallel",), 56),
        name="ple",
    )(x2, p2d, g, w_gate, w_proj)


def kernel(x, p, mix_norm_g, w_in, mlstm_gate_bias, mlstm_out_g, fox_q_g, fox_k_g, fox_f_bias, w_out, ffn_norm_g, w_up, conv_w, conv_b, w_down, ple_norm_g, w_ple_gate, w_ple_proj):
    batch, seq, d_model = x.shape
    depth = w_in.shape[0]
    mh = mlstm_gate_bias.shape[1] // 2
    mw = mlstm_out_g.shape[1]
    md = mw // mh
    fh = fox_f_bias.shape[1]
    fd = fox_q_g.shape[1]
    fw = fh * fd
    n_gate = 2 * mh + fh
    n_rows = -(-n_gate // 8) * 8
    assert mw == fw and fd == LANES and n_rows <= LANES and N_BIAS_TERMS * fh <= LANES
    assert w_in.shape[2] == 4 * mw + 2 * mh + 3 * fw + fh
    tq = _fit(seq, TILES["fox_tq"], LANES)
    tk = _fit(tq, TILES["fox_tk"], LANES)

    xs = x.reshape(batch * seq, d_model)
    for li in range(depth):
        w = w_in[li]
        o_mi = 4 * mw
        o_fq = o_mi + 2 * mh
        o_ff = o_fq + 3 * fw
        w_main = jnp.concatenate([w[:, :mw], w[:, mw:2 * mw] * (md ** -0.5), w[:, 2 * mw:o_mi], w[:, o_fq:o_ff]], axis=1).astype(BF16)
        pad = jnp.zeros((d_model, n_rows - n_gate), F32)
        w_gate = jnp.concatenate([w[:, o_mi:o_fq], pad, w[:, o_ff:], jnp.zeros((d_model, LANES - n_rows), F32)], axis=1).astype(BF16)
        gate_bias = jnp.concatenate([mlstm_gate_bias[li], pad[0], fox_f_bias[li], jnp.zeros((LANES - n_rows,), F32)])[None, :]
        q_gain = (fox_q_g[li] * (fd ** -0.5 * LOG2E))[None, :]
        k_gain = fox_k_g[li][None, :]

        z, zg, vt = _in_proj(xs, mix_norm_g[li][None, :], w_main, w_gate, q_gain, k_gain, batch=batch, tn=mw, jq=4, jk=5, jv=6)
        cols, rows, kb = _gates(zg, gate_bias, batch=batch, seq=seq, blk=tk, n_pass=mh, n_rows=n_rows, n_attn=fh)
        ha = _mlstm(z, rows, cols, mlstm_out_g[li][None, :], batch=batch, seq=seq, heads=mh, d=md)
        hb = _fox(z, kb, vt, rows, batch=batch, seq=seq, heads=fh, d=fd, tq=tq, tk=tk, q_group=4, f_row0=n_rows - fh)

        wo = w_out[li].astype(BF16)
        x1, h2 = _out_proj(xs, ha, hb, wo[:mw], wo[mw:], ffn_norm_g[li][None, :])
        act = _ffn_up(h2, w_up[li].astype(BF16), conv_w[li], conv_b[li][None, :], seq=seq)
        x2 = _ffn_down(act, w_down[li].astype(BF16), x1)
        xs = _ple(x2, p[li].reshape(batch * seq, -1), ple_norm_g[li][None, :], w_ple_gate[li].astype(BF16), w_ple_proj[li].astype(BF16))
    return xs.reshape(batch, seq, d_model)
```

```python
import functools
import math

import jax
import jax.numpy as jnp
from jax import lax
from jax.experimental import pallas as pl
from jax.experimental.pallas import tpu as pltpu

F32 = jnp.float32
BF16 = jnp.bfloat16
EPS = 1e-6
LANES = 128
BF16_SUBLANES = 16
NEG_BIG = -1e30
LOG2E = math.log2(math.e)
N_BIAS_TERMS = 3
F32_MIN_EXP = 127.0
MIB = 1024 * 1024

TILES = dict(
    in_tm=1024, mlstm_chunk=256, fox_tq=1024, fox_tk=512,
    out_tm=512, up_tm=1024, up_tn=512, down_tm=512, down_tn=1024, ple_tm=512,
)


def _fit(n, pref, align):
    t = min(n, pref)
    while t > align and n % t:
        t //= 2
    assert n % t == 0 and (t % align == 0 or t == n), (n, pref, align)
    return t


def _params(semantics, vmem_mib):
    return pltpu.CompilerParams(dimension_semantics=semantics, vmem_limit_bytes=vmem_mib * MIB)


def _rms(x, g):
    return x * lax.rsqrt(jnp.mean(x * x, axis=-1, keepdims=True) + EPS) * g


def _dot(a, b):
    return jnp.dot(a, b, preferred_element_type=F32)


def _dot_nt(a, b):
    return lax.dot_general(a, b, (((1,), (1,)), ((), ())), preferred_element_type=F32)


def _dot_tn(a, b):
    return lax.dot_general(a, b, (((0,), (0,)), ((), ())), preferred_element_type=F32)


def _log_sigmoid(v):
    return jnp.minimum(v, 0.0) - jnp.log1p(jnp.exp(-jnp.abs(v)))


def _in_proj_kernel(x_ref, g_ref, wg_ref, w_ref, qg_ref, kg_ref, z_ref, zg_ref, vt_ref, h_scr, *, jq, jk, jv, fox_d):
    j = pl.program_id(1)

    @pl.when(j == 0)
    def _():
        hb = _rms(x_ref[...], g_ref[...]).astype(BF16)
        h_scr[...] = hb
        zg_ref[...] = _dot(hb, wg_ref[...])

    acc = _dot(h_scr[...], w_ref[...])

    def head_norm(gain_ref):
        for hh in range(acc.shape[1] // fox_d):
            sl = slice(hh * fox_d, (hh + 1) * fox_d)
            z_ref[:, sl] = _rms(acc[:, sl], gain_ref[...]).astype(BF16)

    @pl.when(j == jq)
    def _():
        head_norm(qg_ref)

    @pl.when(j == jk)
    def _():
        head_norm(kg_ref)

    @pl.when(j == jv)
    def _():
        vt_ref[...] = acc.T.astype(BF16)

    @pl.when((j != jq) & (j != jk) & (j != jv))
    def _():
        z_ref[...] = acc.astype(BF16)


def _in_proj(x2d, g, w_main, w_gate, qg, kg, *, batch, tn, jq, jk, jv):
    n, d = x2d.shape
    seq = n // batch
    tm = _fit(seq, TILES["in_tm"], LANES)
    tiles_per_seq = seq // tm
    nt = w_main.shape[1] // tn
    assert jv == nt - 1
    fox_d = qg.shape[1]
    kern = functools.partial(_in_proj_kernel, jq=jq, jk=jk, jv=jv, fox_d=fox_d)
    return pl.pallas_call(
        kern,
        grid=(n // tm, nt),
        in_specs=[
            pl.BlockSpec((tm, d), lambda i, j: (i, 0)),
            pl.BlockSpec((1, d), lambda i, j: (0, 0)),
            pl.BlockSpec((d, LANES), lambda i, j: (0, 0)),
            pl.BlockSpec((d, tn), lambda i, j: (0, j)),
            pl.BlockSpec((1, fox_d), lambda i, j: (0, 0)),
            pl.BlockSpec((1, fox_d), lambda i, j: (0, 0)),
        ],
        out_specs=[
            pl.BlockSpec((tm, tn), lambda i, j: (i, jnp.minimum(j, jv - 1))),
            pl.BlockSpec((tm, LANES), lambda i, j: (i, 0)),
            pl.BlockSpec((None, tn, tm), lambda i, j: (i // tiles_per_seq, 0, i % tiles_per_seq)),
        ],
        out_shape=[
            jax.ShapeDtypeStruct((n, jv * tn), BF16),
            jax.ShapeDtypeStruct((n, LANES), F32),
            jax.ShapeDtypeStruct((batch, tn, seq), BF16),
        ],
        scratch_shapes=[pltpu.VMEM((tm, d), BF16)],
        compiler_params=_params(("parallel", "arbitrary"), 56),
        name="in_proj",
    )(x2d, g, w_gate, w_main, qg, kg)


def _gates_kernel(zg_ref, b_ref, col_ref, row_ref, kb_ref, carry_scr, *, n_pass, n_rows, n_attn):
    @pl.when(pl.program_id(1) == 0)
    def _():
        carry_scr[...] = jnp.zeros_like(carry_scr)

    v = zg_ref[...] + b_ref[...]
    blk = v.shape[0]
    lane = lax.broadcasted_iota(jnp.int32, v.shape, 1)
    summed = lane >= n_pass
    ls = jnp.where(summed, _log_sigmoid(v), 0.0)

    def split3(t):
        hi = t.astype(BF16)
        r1 = t - hi.astype(F32)
        mid = r1.astype(BF16)
        return hi, mid, (r1 - mid.astype(F32)).astype(BF16)

    tri = (lax.broadcasted_iota(jnp.int32, (blk, blk), 0) >= lax.broadcasted_iota(jnp.int32, (blk, blk), 1)).astype(BF16)
    local = sum(_dot(tri, t) for t in split3(ls))
    cs = local + carry_scr[0:1, :]
    carry_scr[...] = jnp.broadcast_to(cs[blk - 1:blk, :], carry_scr.shape)
    out = jnp.where(summed, cs, v)
    col_ref[...] = out
    row_ref[...] = out.T[:n_rows, :]

    a0 = n_rows - n_attn
    src = lax.broadcasted_iota(jnp.int32, (LANES, LANES), 0)
    dst = lax.broadcasted_iota(jnp.int32, (LANES, LANES), 1)
    kb = None
    for c, t in enumerate(split3((local[blk - 1:blk, :] - local) * LOG2E)):
        place = ((src >= a0) & (src < a0 + n_attn) & (dst == N_BIAS_TERMS * (src - a0) + c)).astype(BF16)
        kb = _dot(t, place) if kb is None else kb + _dot(t, place)
    kb_ref[...] = kb.astype(BF16)


def _gates(zg, bias, *, batch, seq, blk, n_pass, n_rows, n_attn):
    nb = seq // blk
    kern = functools.partial(_gates_kernel, n_pass=n_pass, n_rows=n_rows, n_attn=n_attn)
    return pl.pallas_call(
        kern,
        grid=(batch, nb),
        in_specs=[
            pl.BlockSpec((blk, LANES), lambda b, s: (b * nb + s, 0)),
            pl.BlockSpec((1, LANES), lambda b, s: (0, 0)),
        ],
        out_specs=[
            pl.BlockSpec((blk, LANES), lambda b, s: (b * nb + s, 0)),
            pl.BlockSpec((None, n_rows, blk), lambda b, s: (b, 0, s)),
            pl.BlockSpec((blk, LANES), lambda b, s: (b * nb + s, 0)),
        ],
        out_shape=[
            jax.ShapeDtypeStruct((batch * seq, LANES), F32),
            jax.ShapeDtypeStruct((batch, n_rows, seq), F32),
            jax.ShapeDtypeStruct((batch * seq, LANES), BF16),
        ],
        scratch_shapes=[pltpu.VMEM((8, LANES), F32)],
        compiler_params=_params(("arbitrary", "arbitrary"), 32),
        name="gate_prep",
    )(zg, bias)


def _mlstm_kernel(q_ref, k_ref, v_ref, o_ref, row_ref, col_ref, og_ref, out_ref, c_scr, n_scr, m_scr, *, heads, d):
    @pl.when(pl.program_id(1) == 0)
    def _():
        c_scr[...] = jnp.zeros_like(c_scr)
        n_scr[...] = jnp.zeros_like(n_scr)
        m_scr[...] = jnp.zeros_like(m_scr)

    L = q_ref.shape[0]
    causal = lax.broadcasted_iota(jnp.int32, (L, L), 0) >= lax.broadcasted_iota(jnp.int32, (L, L), 1)
    for hh in range(heads):
        sl = slice(hh * d, (hh + 1) * d)
        q, k, v = q_ref[:, sl], k_ref[:, sl], v_ref[:, sl]
        a_row = row_ref[hh:hh + 1, :] - row_ref[heads + hh:heads + hh + 1, :]
        g_col = col_ref[:, heads + hh:heads + hh + 1]
        a_col = col_ref[:, hh:hh + 1] - g_col
        m_old = m_scr[hh, 0:1, 0:1]

        mm = jnp.maximum(m_old, jnp.max(jnp.where(causal, a_row, -jnp.inf), axis=-1, keepdims=True))
        w = jnp.exp(jnp.where(causal, a_row - mm, -jnp.inf))
        s = _dot_nt(q, k) * w
        w_inter = jnp.exp(m_old - mm)
        num = _dot(s.astype(BF16), v) + w_inter * _dot(q, c_scr[hh].astype(BF16))
        qn = jnp.sum(q.astype(F32) * n_scr[hh], axis=-1, keepdims=True)
        den = jnp.sum(s, axis=-1, keepdims=True) + w_inter * qn
        hout = num / jnp.maximum(jnp.abs(den), jnp.exp(-(g_col + mm)))
        gate = jax.nn.sigmoid(o_ref[:, sl].astype(F32))
        out_ref[:, sl] = (_rms(hout, og_ref[:, sl]) * gate).astype(BF16)

        m_new = jnp.maximum(m_old, jnp.max(a_row, axis=-1, keepdims=True))
        decay = jnp.exp(m_old - m_new)
        kw = k.astype(F32) * jnp.exp(a_col - m_new)
        c_scr[hh] = decay * c_scr[hh] + _dot_tn(kw.astype(BF16), v)
        n_scr[hh] = decay * n_scr[hh] + jnp.sum(kw, axis=0, keepdims=True)
        m_scr[hh] = jnp.broadcast_to(m_new, m_scr.shape[1:])


def _mlstm(z, rows, cols, out_g, *, batch, seq, heads, d):
    L = _fit(seq, TILES["mlstm_chunk"], LANES)
    nc = seq // L
    width = heads * d
    n_rows = rows.shape[1]
    kern = functools.partial(_mlstm_kernel, heads=heads, d=d)

    def zspec(group):
        return pl.BlockSpec((L, width), lambda b, c: (b * nc + c, group))

    return pl.pallas_call(
        kern,
        grid=(batch, nc),
        in_specs=[
            zspec(0), zspec(1), zspec(2), zspec(3),
            pl.BlockSpec((None, n_rows, L), lambda b, c: (b, 0, c)),
            pl.BlockSpec((L, LANES), lambda b, c: (b * nc + c, 0)),
            pl.BlockSpec((1, width), lambda b, c: (0, 0)),
        ],
        out_specs=pl.BlockSpec((L, width), lambda b, c: (b * nc + c, 0)),
        out_shape=jax.ShapeDtypeStruct((batch * seq, width), BF16),
        scratch_shapes=[pltpu.VMEM((heads, d, d), F32), pltpu.VMEM((heads, 1, d), F32), pltpu.VMEM((heads, 8, LANES), F32)],
        compiler_params=_params(("parallel", "arbitrary"), 32),
        name="mlstm",
    )(z, z, z, z, rows, cols, out_g)


def _fox_kernel(fend_ref, fq_ref, win_ref, q_ref, k_ref, kb_ref, vt_ref, f_ref, o_ref, qa_scr, s0_scr, s1_scr, acc_scr, *, tq, tk, heads):
    b, h, i = pl.program_id(0), pl.program_id(1), pl.program_id(2)
    n_diag = tq // tk
    q0 = pl.multiple_of(i * tq, tq)
    lane = lax.broadcasted_iota(jnp.int32, (tq, LANES), 1)
    qa_scr[:, :LANES] = q_ref[...]
    qa_scr[:, LANES:] = jnp.where((lane >= N_BIAS_TERMS * h) & (lane < N_BIAS_TERMS * (h + 1)), 1.0, 0.0).astype(BF16)
    acc_scr[...] = jnp.zeros_like(acc_scr)

    def f_end(k0):
        return f_ref[:, pl.ds(pl.multiple_of(k0 + tk - LANES, LANES), LANES)][:, LANES - 1:LANES]

    f_ref_q = f_end(q0 + tq - tk)

    def key_block(k0):
        return jnp.concatenate([k_ref[pl.ds(k0, tk), :], kb_ref[pl.ds(k0, tk), :]], axis=1)

    def scores(k0, s_ref):
        s_ref[...] = _dot_nt(key_block(k0), qa_scr[...])

    def update(k0, s, carry, c0):
        m, l = carry
        cols = slice(c0, None) if c0 else slice(None)
        off = (f_ref_q - f_end(k0)) * LOG2E
        m_new = jnp.maximum(m[:, cols], jnp.max(s, axis=0, keepdims=True) + off)
        alpha = jnp.exp2(m[:, cols] - m_new)
        p = jnp.exp2(s - (m_new - off))
        l_new = alpha * l[:, cols] + jnp.sum(p, axis=0, keepdims=True)
        acc_scr[:, cols] = alpha * acc_scr[:, cols] + _dot(vt_ref[:, pl.ds(k0, tk)], p.astype(BF16))
        if c0:
            m_new = jnp.concatenate([m[:, :c0], m_new], axis=1)
            l_new = jnp.concatenate([l[:, :c0], l_new], axis=1)
        return m_new, l_new

    row = b * heads + h
    thr = fq_ref[row, i] + win_ref[0, 0]
    n_skip = lax.fori_loop(0, i * n_diag, lambda j, c: c + jnp.where(fend_ref[row, j] >= thr, 1, 0), jnp.int32(0))
    t0 = n_skip // 2

    scores(pl.multiple_of(t0 * (2 * tk), 2 * tk), s0_scr)

    def pair(t, carry):
        k0 = pl.multiple_of(t * (2 * tk), 2 * tk)
        scores(k0 + tk, s1_scr)
        carry = update(k0, s0_scr[...], carry, 0)
        scores(k0 + 2 * tk, s0_scr)
        return update(k0 + tk, s1_scr[...], carry, 0)

    carry = (jnp.full((1, tq), NEG_BIG, F32), jnp.zeros((1, tq), F32))
    carry = lax.fori_loop(t0, q0 // (2 * tk), pair, carry)

    slots = (s0_scr, s1_scr)
    tri = lax.broadcasted_iota(jnp.int32, (tk, tk), 0) <= lax.broadcasted_iota(jnp.int32, (tk, tk), 1)
    for jj in range(n_diag):
        if jj + 1 < n_diag:
            scores(q0 + (jj + 1) * tk, slots[(jj + 1) % 2])
        c0 = jj * tk
        s = slots[jj % 2][:, c0:]
        head = jnp.where(tri, s[:, :tk], NEG_BIG)
        s = head if s.shape[1] == tk else jnp.concatenate([head, s[:, tk:]], axis=1)
        carry = update(q0 + c0, s, carry, c0)
    _, l = carry
    o_ref[...] = (acc_scr[...] / l).T.astype(BF16)


def _fox(z, kb, vt, frow, win, *, batch, seq, heads, d, tq, tk, q_group, f_row0):
    assert (tq // tk) % 2 == 0
    nq = seq // tq
    per_group = z.shape[1] // (q_group + 2) // d
    kern = functools.partial(_fox_kernel, tq=tq, tk=tk, heads=heads)
    f4 = frow.reshape(batch, frow.shape[1], 1, seq)
    fh = frow[:, f_row0:f_row0 + heads, :]
    f_end = fh[:, :, tk - 1::tk].reshape(batch * heads, seq // tk)
    f_q = fh[:, :, ::tq].reshape(batch * heads, nq)
    smem = pl.BlockSpec(memory_space=pltpu.SMEM)
    return pl.pallas_call(
        kern,
        grid=(batch, heads, nq),
        in_specs=[
            smem, smem, smem,
            pl.BlockSpec((tq, d), lambda b, h, i: (b * nq + i, q_group * per_group + h)),
            pl.BlockSpec((seq, d), lambda b, h, i: (b, (q_group + 1) * per_group + h)),
            pl.BlockSpec((seq, LANES), lambda b, h, i: (b, 0)),
            pl.BlockSpec((None, d, seq), lambda b, h, i: (b, h, 0)),
            pl.BlockSpec((None, None, 1, seq), lambda b, h, i: (b, f_row0 + h, 0, 0)),
        ],
        out_specs=pl.BlockSpec((tq, d), lambda b, h, i: (b * nq + i, h)),
        out_shape=jax.ShapeDtypeStruct((batch * seq, heads * d), BF16),
        scratch_shapes=[pltpu.VMEM((tq, 2 * LANES), BF16), pltpu.VMEM((tk, tq), F32), pltpu.VMEM((tk, tq), F32), pltpu.VMEM((d, tq), F32)],
        compiler_params=_params(("parallel", "parallel", "arbitrary"), 48),
        name="fox_attention",
    )(f_end, f_q, win, z, z, kb, vt, f4)


def _out_proj_kernel(x_ref, ha_ref, hb_ref, wa_ref, wb_ref, g_ref, x1_ref, h_ref):
    x1 = x_ref[...] + _dot(ha_ref[...], wa_ref[...]) + _dot(hb_ref[...], wb_ref[...])
    x1_ref[...] = x1
    h_ref[...] = _rms(x1, g_ref[...]).astype(BF16)


def _out_proj(x2d, ha, hb, wa, wb, g):
    n, d = x2d.shape
    tm = _fit(n, TILES["out_tm"], BF16_SUBLANES)
    ka, kb = ha.shape[1], hb.shape[1]
    const = dict(pipeline_mode=pl.Buffered(1))
    return pl.pallas_call(
        _out_proj_kernel,
        grid=(n // tm,),
        in_specs=[
            pl.BlockSpec((tm, d), lambda i: (i, 0)),
            pl.BlockSpec((tm, ka), lambda i: (i, 0)),
            pl.BlockSpec((tm, kb), lambda i: (i, 0)),
            pl.BlockSpec((ka, d), lambda i: (0, 0), **const),
            pl.BlockSpec((kb, d), lambda i: (0, 0), **const),
            pl.BlockSpec((1, d), lambda i: (0, 0)),
        ],
        out_specs=[pl.BlockSpec((tm, d), lambda i: (i, 0)), pl.BlockSpec((tm, d), lambda i: (i, 0))],
        out_shape=[jax.ShapeDtypeStruct((n, d), F32), jax.ShapeDtypeStruct((n, d), BF16)],
        compiler_params=_params(("parallel",), 56),
        name="out_proj",
    )(x2d, ha, hb, wa, wb, g)


def _ffn_up_kernel(h_ref, halo_ref, wa_ref, wg_ref, cw_ref, cb_ref, act_ref, a_scr, *, tiles_per_seq, halo):
    i = pl.program_id(0)
    tm = h_ref.shape[0]
    h = h_ref[...]
    a = _dot(h, wa_ref[...])
    a_prev = _dot(halo_ref[...], wa_ref[...])
    a_prev = jnp.where(i % tiles_per_seq == 0, 0.0, a_prev)
    a_scr[0:halo, :] = a_prev
    a_scr[halo:, :] = a
    y = (cw_ref[2:3, :] * a + cw_ref[1:2, :] * a_scr[pl.ds(halo - 1, tm), :]
         + cw_ref[0:1, :] * a_scr[pl.ds(halo - 2, tm), :] + cb_ref[...])
    gelu = 0.5 * y * (1.0 + lax.erf(y * (0.5 ** 0.5)))
    act_ref[...] = (gelu * _dot(h, wg_ref[...])).astype(BF16)


def _ffn_up(h2, w_up, conv_w, conv_b, *, seq):
    n, d = h2.shape
    d_ff = conv_w.shape[1]
    halo = BF16_SUBLANES
    tm = _fit(seq, TILES["up_tm"], halo)
    tn = _fit(d_ff, TILES["up_tn"], LANES)
    nj = d_ff // tn
    kern = functools.partial(_ffn_up_kernel, tiles_per_seq=seq // tm, halo=halo)
    return pl.pallas_call(
        kern,
        grid=(n // tm, nj),
        in_specs=[
            pl.BlockSpec((tm, d), lambda i, j: (i, 0)),
            pl.BlockSpec((halo, d), lambda i, j: (jnp.maximum(i * (tm // halo) - 1, 0), 0)),
            pl.BlockSpec((d, tn), lambda i, j: (0, j)),
            pl.BlockSpec((d, tn), lambda i, j: (0, nj + j)),
            pl.BlockSpec((conv_w.shape[0], tn), lambda i, j: (0, j)),
            pl.BlockSpec((1, tn), lambda i, j: (0, j)),
        ],
        out_specs=pl.BlockSpec((tm, tn), lambda i, j: (i, j)),
        out_shape=jax.ShapeDtypeStruct((n, d_ff), BF16),
        scratch_shapes=[pltpu.VMEM((tm + halo, tn), F32)],
        compiler_params=_params(("parallel", "arbitrary"), 48),
        name="ffn_up",
    )(h2, h2, w_up, w_up, conv_w, conv_b)


def _ffn_down_kernel(act_ref, w_ref, x_ref, o_ref):
    o_ref[...] = x_ref[...] + _dot(act_ref[...], w_ref[...])


def _ffn_down(act, w_down, x1):
    n, d_ff = act.shape
    d = w_down.shape[1]
    tm = _fit(n, TILES["down_tm"], BF16_SUBLANES)
    tn = _fit(d, TILES["down_tn"], LANES)
    return pl.pallas_call(
        _ffn_down_kernel,
        grid=(d // tn, n // tm),
        in_specs=[
            pl.BlockSpec((tm, d_ff), lambda j, i: (i, 0)),
            pl.BlockSpec((d_ff, tn), lambda j, i: (0, j)),
            pl.BlockSpec((tm, tn), lambda j, i: (i, j)),
        ],
        out_specs=pl.BlockSpec((tm, tn), lambda j, i: (i, j)),
        out_shape=jax.ShapeDtypeStruct((n, d), F32),
        compiler_params=_params(("arbitrary", "arbitrary"), 56),
        name="ffn_down",
    )(act, w_down, x1)


def _ple_kernel(x_ref, p_ref, g_ref, wg_ref, wp_ref, o_ref):
    x = x_ref[...]
    gate = jax.nn.sigmoid(_dot(_rms(x, g_ref[...]).astype(BF16), wg_ref[...]))
    o_ref[...] = x + gate * _dot(p_ref[...].astype(BF16), wp_ref[...])


def _ple(x2, p2d, g, w_gate, w_proj):
    n, d = x2.shape
    dp = p2d.shape[1]
    tm = _fit(n, TILES["ple_tm"], 8)
    const = dict(pipeline_mode=pl.Buffered(1))
    return pl.pallas_call(
        _ple_kernel,
        grid=(n // tm,),
        in_specs=[
            pl.BlockSpec((tm, d), lambda i: (i, 0)),
            pl.BlockSpec((tm, dp), lambda i: (i, 0)),
            pl.BlockSpec((1, d), lambda i: (0, 0)),
            pl.BlockSpec((d, d), lambda i: (0, 0), **const),
            pl.BlockSpec((dp, d), lambda i: (0, 0), **const),
        ],
        out_specs=pl.BlockSpec((tm, d), lambda i: (i, 0)),
        out_shape=jax.ShapeDtypeStruct((n, d), F32),
        compiler_params=_params(("parallel",), 56),
        name="ple",
    )(x2, p2d, g, w_gate, w_proj)


def kernel(x, p, mix_norm_g, w_in, mlstm_gate_bias, mlstm_out_g, fox_q_g, fox_k_g, fox_f_bias, w_out, ffn_norm_g, w_up, conv_w, conv_b, w_down, ple_norm_g, w_ple_gate, w_ple_proj):
    batch, seq, d_model = x.shape
    depth = w_in.shape[0]
    mh = mlstm_gate_bias.shape[1] // 2
    mw = mlstm_out_g.shape[1]
    md = mw // mh
    fh = fox_f_bias.shape[1]
    fd = fox_q_g.shape[1]
    fw = fh * fd
    n_gate = 2 * mh + fh
    n_rows = -(-n_gate // 8) * 8
    assert mw == fw and fd == LANES and n_rows <= LANES and N_BIAS_TERMS * fh <= LANES
    assert w_in.shape[2] == 4 * mw + 2 * mh + 3 * fw + fh
    tq = _fit(seq, TILES["fox_tq"], LANES)
    tk = _fit(tq, TILES["fox_tk"], LANES)

    xs = x.reshape(batch * seq, d_model)
    for li in range(depth):
        w = w_in[li]
        o_mi = 4 * mw
        o_fq = o_mi + 2 * mh
        o_ff = o_fq + 3 * fw
        w_main = jnp.concatenate([w[:, :mw], w[:, mw:2 * mw] * (md ** -0.5), w[:, 2 * mw:o_mi], w[:, o_fq:o_ff]], axis=1).astype(BF16)
        pad = jnp.zeros((d_model, n_rows - n_gate), F32)
        w_gate = jnp.concatenate([w[:, o_mi:o_fq], pad, w[:, o_ff:], jnp.zeros((d_model, LANES - n_rows), F32)], axis=1).astype(BF16)
        gate_bias = jnp.concatenate([mlstm_gate_bias[li], pad[0], fox_f_bias[li], jnp.zeros((LANES - n_rows,), F32)])[None, :]
        q_gain = (fox_q_g[li] * (fd ** -0.5 * LOG2E))[None, :]
        k_gain = fox_k_g[li][None, :]

        z, zg, vt = _in_proj(xs, mix_norm_g[li][None, :], w_main, w_gate, q_gain, k_gain, batch=batch, tn=mw, jq=4, jk=5, jv=6)
        cols, rows, kb = _gates(zg, gate_bias, batch=batch, seq=seq, blk=tk, n_pass=mh, n_rows=n_rows, n_attn=fh)
        ha = _mlstm(z, rows, cols, mlstm_out_g[li][None, :], batch=batch, seq=seq, heads=mh, d=md)
        qk_bound = 1.02 * fd * jnp.max(jnp.abs(q_gain)) * jnp.max(jnp.abs(k_gain))
        win = ((2.0 * qk_bound + F32_MIN_EXP) / LOG2E).reshape(1, 1)
        hb = _fox(z, kb, vt, rows, win, batch=batch, seq=seq, heads=fh, d=fd, tq=tq, tk=tk, q_group=4, f_row0=n_rows - fh)

        wo = w_out[li].astype(BF16)
        x1, h2 = _out_proj(xs, ha, hb, wo[:mw], wo[mw:], ffn_norm_g[li][None, :])
        act = _ffn_up(h2, w_up[li].astype(BF16), conv_w[li], conv_b[li][None, :], seq=seq)
        x2 = _ffn_down(act, w_down[li].astype(BF16), x1)
        xs = _ple(x2, p[li].reshape(batch * seq, -1), ple_norm_g[li][None, :], w_ple_gate[li].astype(BF16), w_ple_proj[li].astype(BF16))
    return xs.reshape(batch, seq, d_model)
```

```python
import functools
import math

import jax
import jax.numpy as jnp
from jax import lax
from jax.experimental import pallas as pl
from jax.experimental.pallas import tpu as pltpu

F32 = jnp.float32
BF16 = jnp.bfloat16
EPS = 1e-6
LANES = 128
BF16_SUBLANES = 16
NEG_BIG = -1e30
LOG2E = math.log2(math.e)
N_BIAS_TERMS = 3
F32_MIN_EXP = 127.0
MIB = 1024 * 1024

TILES = dict(
    in_tm=1024, mlstm_chunk=256, fox_tq=1024, fox_tk=512,
    out_tm=512, up_tm=1024, up_tn=512, down_tm=512, down_tn=1024, ple_tm=512,
)


def _fit(n, pref, align):
    t = min(n, pref)
    while t > align and n % t:
        t //= 2
    assert n % t == 0 and (t % align == 0 or t == n), (n, pref, align)
    return t


def _params(semantics, vmem_mib):
    return pltpu.CompilerParams(dimension_semantics=semantics, vmem_limit_bytes=vmem_mib * MIB)


def _rms(x, g):
    return x * lax.rsqrt(jnp.mean(x * x, axis=-1, keepdims=True) + EPS) * g


def _dot(a, b):
    return jnp.dot(a, b, preferred_element_type=F32)


def _dot_nt(a, b):
    return lax.dot_general(a, b, (((1,), (1,)), ((), ())), preferred_element_type=F32)


def _dot_tn(a, b):
    return lax.dot_general(a, b, (((0,), (0,)), ((), ())), preferred_element_type=F32)


def _log_sigmoid(v):
    return jnp.minimum(v, 0.0) - jnp.log1p(jnp.exp(-jnp.abs(v)))


def _proj_plain_kernel(x_ref, g_ref, wg_ref, w_ref, z_ref, zg_ref, h_ref, h_scr):
    @pl.when(pl.program_id(1) == 0)
    def _():
        hb = _rms(x_ref[...], g_ref[...]).astype(BF16)
        h_scr[...] = hb
        h_ref[...] = hb
        zg_ref[...] = _dot(hb, wg_ref[...])

    z_ref[...] = _dot(h_scr[...], w_ref[...]).astype(BF16)


def _proj_plain(x2d, g, w, w_gate, *, tn):
    n, d = x2d.shape
    tm = _fit(n, TILES["in_tm"], LANES)
    return pl.pallas_call(
        _proj_plain_kernel,
        grid=(n // tm, w.shape[1] // tn),
        in_specs=[
            pl.BlockSpec((tm, d), lambda i, j: (i, 0)),
            pl.BlockSpec((1, d), lambda i, j: (0, 0)),
            pl.BlockSpec((d, LANES), lambda i, j: (0, 0)),
            pl.BlockSpec((d, tn), lambda i, j: (0, j)),
        ],
        out_specs=[
            pl.BlockSpec((tm, tn), lambda i, j: (i, j)),
            pl.BlockSpec((tm, LANES), lambda i, j: (i, 0)),
            pl.BlockSpec((tm, d), lambda i, j: (i, 0)),
        ],
        out_shape=[
            jax.ShapeDtypeStruct((n, w.shape[1]), BF16),
            jax.ShapeDtypeStruct((n, LANES), F32),
            jax.ShapeDtypeStruct((n, d), BF16),
        ],
        scratch_shapes=[pltpu.VMEM((tm, d), BF16)],
        compiler_params=_params(("parallel", "arbitrary"), 56),
        name="proj_plain",
    )(x2d, g, w_gate, w)


def _proj_headnorm_kernel(h_ref, w_ref, gain_ref, z_ref):
    acc = _dot(h_ref[...], w_ref[...])
    hd = gain_ref.shape[-1]
    for hh in range(acc.shape[1] // hd):
        sl = slice(hh * hd, (hh + 1) * hd)
        z_ref[:, sl] = _rms(acc[:, sl], gain_ref[...]).astype(BF16)


def _proj_headnorm(h, w, gains, *, tn):
    n, d = h.shape
    hd = gains.shape[-1]
    tm = _fit(n, TILES["in_tm"], LANES)
    return pl.pallas_call(
        _proj_headnorm_kernel,
        grid=(n // tm, w.shape[1] // tn),
        in_specs=[
            pl.BlockSpec((tm, d), lambda i, j: (i, 0)),
            pl.BlockSpec((d, tn), lambda i, j: (0, j)),
            pl.BlockSpec((None, 1, hd), lambda i, j: (j, 0, 0)),
        ],
        out_specs=pl.BlockSpec((tm, tn), lambda i, j: (i, j)),
        out_shape=jax.ShapeDtypeStruct((n, w.shape[1]), BF16),
        compiler_params=_params(("parallel", "arbitrary"), 56),
        name="proj_headnorm",
    )(h, w, gains)


def _proj_transposed_kernel(h_ref, w_ref, vt_ref):
    vt_ref[...] = lax.dot_general(w_ref[...], h_ref[...], (((0,), (1,)), ((), ())), preferred_element_type=F32).astype(BF16)


def _proj_transposed(h, w, *, batch):
    n, d = h.shape
    seq = n // batch
    tn = w.shape[1]
    tm = _fit(seq, TILES["in_tm"], LANES)
    tiles_per_seq = seq // tm
    return pl.pallas_call(
        _proj_transposed_kernel,
        grid=(n // tm,),
        in_specs=[
            pl.BlockSpec((tm, d), lambda i: (i, 0)),
            pl.BlockSpec((d, tn), lambda i: (0, 0), pipeline_mode=pl.Buffered(1)),
        ],
        out_specs=pl.BlockSpec((None, tn, tm), lambda i: (i // tiles_per_seq, 0, i % tiles_per_seq)),
        out_shape=jax.ShapeDtypeStruct((batch, tn, seq), BF16),
        compiler_params=_params(("parallel",), 56),
        name="proj_transposed",
    )(h, w)


def _gates_kernel(zg_ref, b_ref, row_ref, kb_ref, ab_ref, carry_scr, *, n_pass, n_rows, n_attn):
    @pl.when(pl.program_id(1) == 0)
    def _():
        carry_scr[...] = jnp.zeros_like(carry_scr)

    v = zg_ref[...] + b_ref[...]
    blk = v.shape[0]
    lane = lax.broadcasted_iota(jnp.int32, v.shape, 1)
    summed = lane >= n_pass
    ls = jnp.where(summed, _log_sigmoid(v), 0.0)

    def split3(t):
        hi = t.astype(BF16)
        r1 = t - hi.astype(F32)
        mid = r1.astype(BF16)
        return hi, mid, (r1 - mid.astype(F32)).astype(BF16)

    tri = (lax.broadcasted_iota(jnp.int32, (blk, blk), 0) >= lax.broadcasted_iota(jnp.int32, (blk, blk), 1)).astype(BF16)
    local = sum(_dot(tri, t) for t in split3(ls))
    cs = local + carry_scr[0:1, :]
    carry_scr[...] = jnp.broadcast_to(cs[blk - 1:blk, :], carry_scr.shape)
    out = jnp.where(summed, cs, v)
    row_ref[...] = out.T[:n_rows, :]

    a0 = n_rows - n_attn
    src = lax.broadcasted_iota(jnp.int32, (LANES, LANES), 0)
    dst = lax.broadcasted_iota(jnp.int32, (LANES, LANES), 1)
    kb = None
    for c, t in enumerate(split3((local[blk - 1:blk, :] - local) * LOG2E)):
        place = ((src >= a0) & (src < a0 + n_attn) & (dst == N_BIAS_TERMS * (src - a0) + c)).astype(BF16)
        kb = _dot(t, place) if kb is None else kb + _dot(t, place)
    kb_ref[...] = kb.astype(BF16)

    ab = None
    for c, t in enumerate(split3(out - pltpu.roll(out, LANES - n_pass, axis=1))):
        place = ((src < n_pass) & (dst == N_BIAS_TERMS * src + c)).astype(BF16)
        ab = _dot(t, place) if ab is None else ab + _dot(t, place)
    ab_ref[...] = ab.astype(BF16)


def _gates(zg, bias, *, batch, seq, blk, n_pass, n_rows, n_attn):
    nb = seq // blk
    kern = functools.partial(_gates_kernel, n_pass=n_pass, n_rows=n_rows, n_attn=n_attn)
    return pl.pallas_call(
        kern,
        grid=(batch, nb),
        in_specs=[
            pl.BlockSpec((blk, LANES), lambda b, s: (b * nb + s, 0)),
            pl.BlockSpec((1, LANES), lambda b, s: (0, 0)),
        ],
        out_specs=[
            pl.BlockSpec((None, n_rows, blk), lambda b, s: (b, 0, s)),
            pl.BlockSpec((blk, LANES), lambda b, s: (b * nb + s, 0)),
            pl.BlockSpec((blk, LANES), lambda b, s: (b * nb + s, 0)),
        ],
        out_shape=[
            jax.ShapeDtypeStruct((batch, n_rows, seq), F32),
            jax.ShapeDtypeStruct((batch * seq, LANES), BF16),
            jax.ShapeDtypeStruct((batch * seq, LANES), BF16),
        ],
        scratch_shapes=[pltpu.VMEM((8, LANES), F32)],
        compiler_params=_params(("arbitrary", "arbitrary"), 32),
        name="gate_prep",
    )(zg, bias)


def _mlstm_kernel(q_ref, k_ref, o_ref, vt_ref, row_ref, ab_ref, og_ref, out_ref, ct_scr, n_scr, m_scr, *, heads, d):
    @pl.when(pl.program_id(1) == 0)
    def _():
        ct_scr[...] = jnp.zeros_like(ct_scr)
        n_scr[...] = jnp.zeros_like(n_scr)
        m_scr[...] = jnp.zeros_like(m_scr)

    L = q_ref.shape[0]
    below = lax.broadcasted_iota(jnp.int32, (L, L), 0) <= lax.broadcasted_iota(jnp.int32, (L, L), 1)
    term = lax.broadcasted_iota(jnp.int32, (LANES, L), 0)
    for hh in range(heads):
        sl = slice(hh * d, (hh + 1) * d)
        q, k, vt = q_ref[:, sl], k_ref[:, sl], vt_ref[sl, :]
        pick = jnp.where((term >= N_BIAS_TERMS * hh) & (term < N_BIAS_TERMS * (hh + 1)), 1.0, 0.0).astype(BF16)
        a_bc = _dot(ab_ref[...], pick)
        g_row = row_ref[heads + hh:heads + hh + 1, :]
        m_old = m_scr[hh, 0:1, 0:1]

        mm = jnp.maximum(m_old, jnp.max(jnp.where(below, a_bc, -jnp.inf), axis=0, keepdims=True))
        w = jnp.exp(jnp.where(below, a_bc - mm, -jnp.inf))
        st = _dot_nt(k, q) * w
        w_inter = jnp.exp(m_old - mm)
        num = _dot(vt, st.astype(BF16)) + w_inter * _dot_nt(ct_scr[hh].astype(BF16), q)
        nq = _dot_nt(jnp.broadcast_to(n_scr[hh], (8, d)).astype(BF16), q)[0:1, :]
        den = jnp.sum(st, axis=0, keepdims=True) + w_inter * nq
        hout = num * (1.0 / jnp.maximum(jnp.abs(den), jnp.exp(-(g_row + mm))))
        inv_rms = lax.rsqrt(jnp.mean(hout * hout, axis=0, keepdims=True) + EPS)
        gain = jnp.concatenate([og_ref[sl, :]] * (L // LANES), axis=1)
        gate = jax.nn.sigmoid(o_ref[:, sl].astype(F32))
        out_ref[:, sl] = ((hout * inv_rms * gain).T * gate).astype(BF16)

        m_new = mm[:, L - 1:L]
        decay = jnp.exp(m_old - m_new)
        kw = k.astype(F32) * jnp.exp(a_bc[:, :d] - m_new)
        ct_scr[hh] = decay * ct_scr[hh] + _dot(vt, kw.astype(BF16))
        n_scr[hh] = decay * n_scr[hh] + jnp.sum(kw, axis=0, keepdims=True)
        m_scr[hh] = jnp.broadcast_to(m_new, m_scr.shape[1:])


def _mlstm(z, vt, rows, ab, out_g, *, batch, seq, heads, d, vt_block):
    L = _fit(seq, TILES["mlstm_chunk"], LANES)
    assert L == d
    nc = seq // L
    width = heads * d
    n_rows = rows.shape[1]
    kern = functools.partial(_mlstm_kernel, heads=heads, d=d)
    og_b = jnp.broadcast_to(out_g.reshape(width, 1), (width, LANES))

    def zspec(group):
        return pl.BlockSpec((L, width), lambda b, c: (b * nc + c, group))

    return pl.pallas_call(
        kern,
        grid=(batch, nc),
        in_specs=[
            zspec(0), zspec(1), zspec(2),
            pl.BlockSpec((None, width, L), lambda b, c: (b, vt_block, c)),
            pl.BlockSpec((None, n_rows, L), lambda b, c: (b, 0, c)),
            pl.BlockSpec((L, LANES), lambda b, c: (b * nc + c, 0)),
            pl.BlockSpec((width, LANES), lambda b, c: (0, 0)),
        ],
        out_specs=pl.BlockSpec((L, width), lambda b, c: (b * nc + c, 0)),
        out_shape=jax.ShapeDtypeStruct((batch * seq, width), BF16),
        scratch_shapes=[pltpu.VMEM((heads, d, d), F32), pltpu.VMEM((heads, 1, d), F32), pltpu.VMEM((heads, 8, LANES), F32)],
        compiler_params=_params(("parallel", "arbitrary"), 32),
        name="mlstm",
    )(z, z, z, vt, rows, ab, og_b)


def _fox_kernel(fend_ref, fq_ref, win_ref, q_ref, k_ref, kb_ref, vt_ref, f_ref, o_ref, qa_scr, s0_scr, s1_scr, acc_scr, *, tq, tk, heads):
    b, h, i = pl.program_id(0), pl.program_id(1), pl.program_id(2)
    n_diag = tq // tk
    q0 = pl.multiple_of(i * tq, tq)
    lane = lax.broadcasted_iota(jnp.int32, (tq, LANES), 1)
    qa_scr[:, :LANES] = q_ref[...]
    qa_scr[:, LANES:] = jnp.where((lane >= N_BIAS_TERMS * h) & (lane < N_BIAS_TERMS * (h + 1)), 1.0, 0.0).astype(BF16)
    acc_scr[...] = jnp.zeros_like(acc_scr)

    def f_end(k0):
        return f_ref[:, pl.ds(pl.multiple_of(k0 + tk - LANES, LANES), LANES)][:, LANES - 1:LANES]

    f_ref_q = f_end(q0 + tq - tk)

    def key_block(k0):
        return jnp.concatenate([k_ref[pl.ds(k0, tk), :], kb_ref[pl.ds(k0, tk), :]], axis=1)

    def scores(k0, s_ref):
        s_ref[...] = _dot_nt(key_block(k0), qa_scr[...])

    def update(k0, s, carry, c0):
        m, l = carry
        cols = slice(c0, None) if c0 else slice(None)
        off = (f_ref_q - f_end(k0)) * LOG2E
        m_new = jnp.maximum(m[:, cols], jnp.max(s, axis=0, keepdims=True) + off)
        alpha = jnp.exp2(m[:, cols] - m_new)
        p = jnp.exp2(s - (m_new - off))
        l_new = alpha * l[:, cols] + jnp.sum(p, axis=0, keepdims=True)
        acc_scr[:, cols] = alpha * acc_scr[:, cols] + _dot(vt_ref[:, pl.ds(k0, tk)], p.astype(BF16))
        if c0:
            m_new = jnp.concatenate([m[:, :c0], m_new], axis=1)
            l_new = jnp.concatenate([l[:, :c0], l_new], axis=1)
        return m_new, l_new

    row = b * heads + h
    thr = fq_ref[row, i] + win_ref[0, 0]
    n_skip = lax.fori_loop(0, i * n_diag, lambda j, c: c + jnp.where(fend_ref[row, j] >= thr, 1, 0), jnp.int32(0))
    t0 = n_skip // 2

    scores(pl.multiple_of(t0 * (2 * tk), 2 * tk), s0_scr)

    def pair(t, carry):
        k0 = pl.multiple_of(t * (2 * tk), 2 * tk)
        scores(k0 + tk, s1_scr)
        carry = update(k0, s0_scr[...], carry, 0)
        scores(k0 + 2 * tk, s0_scr)
        return update(k0 + tk, s1_scr[...], carry, 0)

    carry = (jnp.full((1, tq), NEG_BIG, F32), jnp.zeros((1, tq), F32))
    carry = lax.fori_loop(t0, q0 // (2 * tk), pair, carry)

    slots = (s0_scr, s1_scr)
    tri = lax.broadcasted_iota(jnp.int32, (tk, tk), 0) <= lax.broadcasted_iota(jnp.int32, (tk, tk), 1)
    for jj in range(n_diag):
        if jj + 1 < n_diag:
            scores(q0 + (jj + 1) * tk, slots[(jj + 1) % 2])
        c0 = jj * tk
        s = slots[jj % 2][:, c0:]
        head = jnp.where(tri, s[:, :tk], NEG_BIG)
        s = head if s.shape[1] == tk else jnp.concatenate([head, s[:, tk:]], axis=1)
        carry = update(q0 + c0, s, carry, c0)
    _, l = carry
    o_ref[...] = (acc_scr[...] / l).T.astype(BF16)


def _fox(z, kb, vt, frow, win, *, batch, seq, heads, d, tq, tk, q_group, f_row0):
    assert (tq // tk) % 2 == 0
    nq = seq // tq
    per_group = z.shape[1] // (q_group + 2) // d
    kern = functools.partial(_fox_kernel, tq=tq, tk=tk, heads=heads)
    f4 = frow.reshape(batch, frow.shape[1], 1, seq)
    fh = frow[:, f_row0:f_row0 + heads, :]
    f_end = fh[:, :, tk - 1::tk].reshape(batch * heads, seq // tk)
    f_q = fh[:, :, ::tq].reshape(batch * heads, nq)
    smem = pl.BlockSpec(memory_space=pltpu.SMEM)
    return pl.pallas_call(
        kern,
        grid=(batch, heads, nq),
        in_specs=[
            smem, smem, smem,
            pl.BlockSpec((tq, d), lambda b, h, i: (b * nq + i, q_group * per_group + h)),
            pl.BlockSpec((seq, d), lambda b, h, i: (b, (q_group + 1) * per_group + h)),
            pl.BlockSpec((seq, LANES), lambda b, h, i: (b, 0)),
            pl.BlockSpec((None, d, seq), lambda b, h, i: (b, h, 0)),
            pl.BlockSpec((None, None, 1, seq), lambda b, h, i: (b, f_row0 + h, 0, 0)),
        ],
        out_specs=pl.BlockSpec((tq, d), lambda b, h, i: (b * nq + i, h)),
        out_shape=jax.ShapeDtypeStruct((batch * seq, heads * d), BF16),
        scratch_shapes=[pltpu.VMEM((tq, 2 * LANES), BF16), pltpu.VMEM((tk, tq), F32), pltpu.VMEM((tk, tq), F32), pltpu.VMEM((d, tq), F32)],
        compiler_params=_params(("parallel", "parallel", "arbitrary"), 48),
        name="fox_attention",
    )(f_end, f_q, win, z, z, kb, vt, f4)


def _out_proj_kernel(x_ref, ha_ref, hb_ref, wa_ref, wb_ref, g_ref, x1_ref, h_ref):
    x1 = x_ref[...] + _dot(ha_ref[...], wa_ref[...]) + _dot(hb_ref[...], wb_ref[...])
    x1_ref[...] = x1
    h_ref[...] = _rms(x1, g_ref[...]).astype(BF16)


def _out_proj(x2d, ha, hb, wa, wb, g):
    n, d = x2d.shape
    tm = _fit(n, TILES["out_tm"], BF16_SUBLANES)
    ka, kb = ha.shape[1], hb.shape[1]
    const = dict(pipeline_mode=pl.Buffered(1))
    return pl.pallas_call(
        _out_proj_kernel,
        grid=(n // tm,),
        in_specs=[
            pl.BlockSpec((tm, d), lambda i: (i, 0)),
            pl.BlockSpec((tm, ka), lambda i: (i, 0)),
            pl.BlockSpec((tm, kb), lambda i: (i, 0)),
            pl.BlockSpec((ka, d), lambda i: (0, 0), **const),
            pl.BlockSpec((kb, d), lambda i: (0, 0), **const),
            pl.BlockSpec((1, d), lambda i: (0, 0)),
        ],
        out_specs=[pl.BlockSpec((tm, d), lambda i: (i, 0)), pl.BlockSpec((tm, d), lambda i: (i, 0))],
        out_shape=[jax.ShapeDtypeStruct((n, d), F32), jax.ShapeDtypeStruct((n, d), BF16)],
        compiler_params=_params(("parallel",), 56),
        name="out_proj",
    )(x2d, ha, hb, wa, wb, g)


def _ffn_up_kernel(h_ref, halo_ref, wa_ref, wg_ref, cw_ref, cb_ref, act_ref, a_scr, *, tiles_per_seq, halo):
    i = pl.program_id(0)
    tm = h_ref.shape[0]
    h = h_ref[...]
    a = _dot(h, wa_ref[...])
    a_prev = _dot(halo_ref[...], wa_ref[...])
    a_prev = jnp.where(i % tiles_per_seq == 0, 0.0, a_prev)
    a_scr[0:halo, :] = a_prev
    a_scr[halo:, :] = a
    y = (cw_ref[2:3, :] * a + cw_ref[1:2, :] * a_scr[pl.ds(halo - 1, tm), :]
         + cw_ref[0:1, :] * a_scr[pl.ds(halo - 2, tm), :] + cb_ref[...])
    gelu = 0.5 * y * (1.0 + lax.erf(y * (0.5 ** 0.5)))
    act_ref[...] = (gelu * _dot(h, wg_ref[...])).astype(BF16)


def _ffn_up(h2, w_up, conv_w, conv_b, *, seq):
    n, d = h2.shape
    d_ff = conv_w.shape[1]
    halo = BF16_SUBLANES
    tm = _fit(seq, TILES["up_tm"], halo)
    tn = _fit(d_ff, TILES["up_tn"], LANES)
    nj = d_ff // tn
    kern = functools.partial(_ffn_up_kernel, tiles_per_seq=seq // tm, halo=halo)
    return pl.pallas_call(
        kern,
        grid=(n // tm, nj),
        in_specs=[
            pl.BlockSpec((tm, d), lambda i, j: (i, 0)),
            pl.BlockSpec((halo, d), lambda i, j: (jnp.maximum(i * (tm // halo) - 1, 0), 0)),
            pl.BlockSpec((d, tn), lambda i, j: (0, j)),
            pl.BlockSpec((d, tn), lambda i, j: (0, nj + j)),
            pl.BlockSpec((conv_w.shape[0], tn), lambda i, j: (0, j)),
            pl.BlockSpec((1, tn), lambda i, j: (0, j)),
        ],
        out_specs=pl.BlockSpec((tm, tn), lambda i, j: (i, j)),
        out_shape=jax.ShapeDtypeStruct((n, d_ff), BF16),
        scratch_shapes=[pltpu.VMEM((tm + halo, tn), F32)],
        compiler_params=_params(("parallel", "arbitrary"), 48),
        name="ffn_up",
    )(h2, h2, w_up, w_up, conv_w, conv_b)


def _ffn_down_kernel(act_ref, w_ref, x_ref, o_ref):
    o_ref[...] = x_ref[...] + _dot(act_ref[...], w_ref[...])


def _ffn_down(act, w_down, x1):
    n, d_ff = act.shape
    d = w_down.shape[1]
    tm = _fit(n, TILES["down_tm"], BF16_SUBLANES)
    tn = _fit(d, TILES["down_tn"], LANES)
    return pl.pallas_call(
        _ffn_down_kernel,
        grid=(d // tn, n // tm),
        in_specs=[
            pl.BlockSpec((tm, d_ff), lambda j, i: (i, 0)),
            pl.BlockSpec((d_ff, tn), lambda j, i: (0, j)),
            pl.BlockSpec((tm, tn), lambda j, i: (i, j)),
        ],
        out_specs=pl.BlockSpec((tm, tn), lambda j, i: (i, j)),
        out_shape=jax.ShapeDtypeStruct((n, d), F32),
        compiler_params=_params(("arbitrary", "arbitrary"), 56),
        name="ffn_down",
    )(act, w_down, x1)


def _ple_kernel(x_ref, p_ref, g_ref, wg_ref, wp_ref, o_ref):
    x = x_ref[...]
    gate = jax.nn.sigmoid(_dot(_rms(x, g_ref[...]).astype(BF16), wg_ref[...]))
    o_ref[...] = x + gate * _dot(p_ref[...].astype(BF16), wp_ref[...])


def _ple(x2, p2d, g, w_gate, w_proj):
    n, d = x2.shape
    dp = p2d.shape[1]
    tm = _fit(n, TILES["ple_tm"], 8)
    const = dict(pipeline_mode=pl.Buffered(1))
    return pl.pallas_call(
        _ple_kernel,
        grid=(n // tm,),
        in_specs=[
            pl.BlockSpec((tm, d), lambda i: (i, 0)),
            pl.BlockSpec((tm, dp), lambda i: (i, 0)),
            pl.BlockSpec((1, d), lambda i: (0, 0)),
            pl.BlockSpec((d, d), lambda i: (0, 0), **const),
            pl.BlockSpec((dp, d), lambda i: (0, 0), **const),
        ],
        out_specs=pl.BlockSpec((tm, d), lambda i: (i, 0)),
        out_shape=jax.ShapeDtypeStruct((n, d), F32),
        compiler_params=_params(("parallel",), 56),
        name="ple",
    )(x2, p2d, g, w_gate, w_proj)


def kernel(x, p, mix_norm_g, w_in, mlstm_gate_bias, mlstm_out_g, fox_q_g, fox_k_g, fox_f_bias, w_out, ffn_norm_g, w_up, conv_w, conv_b, w_down, ple_norm_g, w_ple_gate, w_ple_proj):
    batch, seq, d_model = x.shape
    depth = w_in.shape[0]
    mh = mlstm_gate_bias.shape[1] // 2
    mw = mlstm_out_g.shape[1]
    md = mw // mh
    fh = fox_f_bias.shape[1]
    fd = fox_q_g.shape[1]
    fw = fh * fd
    n_gate = 2 * mh + fh
    n_rows = -(-n_gate // 8) * 8
    assert mw == fw and fd == LANES and n_rows <= LANES and N_BIAS_TERMS * fh <= LANES
    assert w_in.shape[2] == 4 * mw + 2 * mh + 3 * fw + fh
    tq = _fit(seq, TILES["fox_tq"], LANES)
    tk = _fit(tq, TILES["fox_tk"], LANES)

    xs = x.reshape(batch * seq, d_model)
    for li in range(depth):
        w = w_in[li]
        o_mi = 4 * mw
        o_fq = o_mi + 2 * mh
        o_ff = o_fq + 3 * fw
        w_plain = jnp.concatenate([w[:, :mw], w[:, mw:2 * mw] * (md ** -0.5), w[:, 3 * mw:o_mi]], axis=1).astype(BF16)
        w_qk = w[:, o_fq:o_fq + 2 * fw].astype(BF16)
        w_v = jnp.concatenate([w[:, o_fq + 2 * fw:o_ff], w[:, 2 * mw:3 * mw]], axis=1).astype(BF16)
        pad = jnp.zeros((d_model, n_rows - n_gate), F32)
        w_gate = jnp.concatenate([w[:, o_mi:o_fq], pad, w[:, o_ff:], jnp.zeros((d_model, LANES - n_rows), F32)], axis=1).astype(BF16)
        gate_bias = jnp.concatenate([mlstm_gate_bias[li], pad[0], fox_f_bias[li], jnp.zeros((LANES - n_rows,), F32)])[None, :]
        q_gain = (fox_q_g[li] * (fd ** -0.5 * LOG2E))[None, :]
        k_gain = fox_k_g[li][None, :]

        zm, zg, h1 = _proj_plain(xs, mix_norm_g[li][None, :], w_plain, w_gate, tn=mw)
        zf = _proj_headnorm(h1, w_qk, jnp.stack([q_gain, k_gain]), tn=fw)
        vt = _proj_transposed(h1, w_v, batch=batch)
        rows, kb, ab = _gates(zg, gate_bias, batch=batch, seq=seq, blk=tk, n_pass=mh, n_rows=n_rows, n_attn=fh)
        ha = _mlstm(zm, vt, rows, ab, mlstm_out_g[li], batch=batch, seq=seq, heads=mh, d=md, vt_block=1)
        qk_bound = 1.02 * fd * jnp.max(jnp.abs(q_gain)) * jnp.max(jnp.abs(k_gain))
        win = ((2.0 * qk_bound + F32_MIN_EXP) / LOG2E).reshape(1, 1)
        hb = _fox(zf, kb, vt, rows, win, batch=batch, seq=seq, heads=fh, d=fd, tq=tq, tk=tk, q_group=0, f_row0=n_rows - fh)

        wo = w_out[li].astype(BF16)
        x1, h2 = _out_proj(xs, ha, hb, wo[:mw], wo[mw:], ffn_norm_g[li][None, :])
        act = _ffn_up(h2, w_up[li].astype(BF16), conv_w[li], conv_b[li][None, :], seq=seq)
        x2 = _ffn_down(act, w_down[li].astype(BF16), x1)
        xs = _ple(x2, p[li].reshape(batch * seq, -1), ple_norm_g[li][None, :], w_ple_gate[li].astype(BF16), w_ple_proj[li].astype(BF16))
    return xs.reshape(batch, seq, d_model)
```

```python
import functools
import math

import jax
import jax.numpy as jnp
from jax import lax
from jax.experimental import pallas as pl
from jax.experimental.pallas import tpu as pltpu

F32 = jnp.float32
BF16 = jnp.bfloat16
EPS = 1e-6
LANES = 128
BF16_SUBLANES = 16
NEG_BIG = -1e30
LOG2E = math.log2(math.e)
N_BIAS_TERMS = 3
F32_MIN_EXP = 127.0
MIB = 1024 * 1024

TILES = dict(
    regroup_cols=256, in_tm=1024, mlstm_chunk=256, fox_tq=1024, fox_tk=512,
    out_tm=512, up_tm=1024, up_tn=512, down_tm=512, down_tn=1024, ple_tm=512,
)


def _fit(n, pref, align):
    t = min(n, pref)
    while t > align and n % t:
        t //= 2
    assert n % t == 0 and (t % align == 0 or t == n), (n, pref, align)
    return t


def _params(semantics, vmem_mib):
    return pltpu.CompilerParams(dimension_semantics=semantics, vmem_limit_bytes=vmem_mib * MIB)


def _rms(x, g):
    return x * lax.rsqrt(jnp.mean(x * x, axis=-1, keepdims=True) + EPS) * g


def _dot(a, b):
    return jnp.dot(a, b, preferred_element_type=F32)


def _dot_nt(a, b):
    return lax.dot_general(a, b, (((1,), (1,)), ((), ())), preferred_element_type=F32)


def _dot_tn(a, b):
    return lax.dot_general(a, b, (((0,), (0,)), ((), ())), preferred_element_type=F32)


def _log_sigmoid(v):
    return jnp.minimum(v, 0.0) - jnp.log1p(jnp.exp(-jnp.abs(v)))


def _proj_plain_kernel(x_ref, g_ref, wg_ref, w_ref, z_ref, zg_ref, h_ref, h_scr):
    @pl.when(pl.program_id(1) == 0)
    def _():
        hb = _rms(x_ref[...], g_ref[...]).astype(BF16)
        h_scr[...] = hb
        h_ref[...] = hb
        zg_ref[...] = _dot(hb, wg_ref[...])

    z_ref[...] = _dot(h_scr[...], w_ref[...]).astype(BF16)


def _proj_plain(x2d, g, w, w_gate, *, tn):
    n, d = x2d.shape
    tm = _fit(n, TILES["in_tm"], LANES)
    return pl.pallas_call(
        _proj_plain_kernel,
        grid=(n // tm, w.shape[1] // tn),
        in_specs=[
            pl.BlockSpec((tm, d), lambda i, j: (i, 0)),
            pl.BlockSpec((1, d), lambda i, j: (0, 0)),
            pl.BlockSpec((d, LANES), lambda i, j: (0, 0)),
            pl.BlockSpec((d, tn), lambda i, j: (0, j)),
        ],
        out_specs=[
            pl.BlockSpec((tm, tn), lambda i, j: (i, j)),
            pl.BlockSpec((tm, LANES), lambda i, j: (i, 0)),
            pl.BlockSpec((tm, d), lambda i, j: (i, 0)),
        ],
        out_shape=[
            jax.ShapeDtypeStruct((n, w.shape[1]), BF16),
            jax.ShapeDtypeStruct((n, LANES), F32),
            jax.ShapeDtypeStruct((n, d), BF16),
        ],
        scratch_shapes=[pltpu.VMEM((tm, d), BF16)],
        compiler_params=_params(("parallel", "arbitrary"), 56),
        name="proj_plain",
    )(x2d, g, w_gate, w)


def _proj_headnorm_kernel(h_ref, w_ref, gain_ref, z_ref):
    acc = _dot(h_ref[...], w_ref[...])
    hd = gain_ref.shape[-1]
    for hh in range(acc.shape[1] // hd):
        sl = slice(hh * hd, (hh + 1) * hd)
        z_ref[:, sl] = _rms(acc[:, sl], gain_ref[...]).astype(BF16)


def _proj_headnorm(h, w, gains, *, tn):
    n, d = h.shape
    hd = gains.shape[-1]
    tm = _fit(n, TILES["in_tm"], LANES)
    return pl.pallas_call(
        _proj_headnorm_kernel,
        grid=(n // tm, w.shape[1] // tn),
        in_specs=[
            pl.BlockSpec((tm, d), lambda i, j: (i, 0)),
            pl.BlockSpec((d, tn), lambda i, j: (0, j)),
            pl.BlockSpec((None, 1, hd), lambda i, j: (j, 0, 0)),
        ],
        out_specs=pl.BlockSpec((tm, tn), lambda i, j: (i, j)),
        out_shape=jax.ShapeDtypeStruct((n, w.shape[1]), BF16),
        compiler_params=_params(("parallel", "arbitrary"), 56),
        name="proj_headnorm",
    )(h, w, gains)


def _proj_transposed_kernel(h_ref, w_ref, vt_ref):
    vt_ref[...] = lax.dot_general(w_ref[...], h_ref[...], (((0,), (1,)), ((), ())), preferred_element_type=F32).astype(BF16)


def _proj_transposed(h, w, *, batch):
    n, d = h.shape
    seq = n // batch
    tn = w.shape[1]
    tm = _fit(seq, TILES["in_tm"], LANES)
    tiles_per_seq = seq // tm
    return pl.pallas_call(
        _proj_transposed_kernel,
        grid=(n // tm,),
        in_specs=[
            pl.BlockSpec((tm, d), lambda i: (i, 0)),
            pl.BlockSpec((d, tn), lambda i: (0, 0), pipeline_mode=pl.Buffered(1)),
        ],
        out_specs=pl.BlockSpec((None, tn, tm), lambda i: (i // tiles_per_seq, 0, i % tiles_per_seq)),
        out_shape=jax.ShapeDtypeStruct((batch, tn, seq), BF16),
        compiler_params=_params(("parallel",), 56),
        name="proj_transposed",
    )(h, w)


def _regroup_kernel(wt_ref, plain_ref, qk_ref, v_ref, gate_ref, *, mw, fw, mh, fh, k_scale, n_rows):
    o_mi = 4 * mw
    o_fq = o_mi + 2 * mh
    o_ff = o_fq + 3 * fw

    def put(dst, c0, r0, r1, scale=None):
        t = wt_ref[r0:r1, :].T
        dst[:, c0:c0 + (r1 - r0)] = (t if scale is None else t * scale).astype(BF16)

    put(plain_ref, 0, 0, mw)
    put(plain_ref, mw, mw, 2 * mw, k_scale)
    put(plain_ref, 2 * mw, 3 * mw, o_mi)
    put(qk_ref, 0, o_fq, o_fq + 2 * fw)
    put(v_ref, 0, o_fq + 2 * fw, o_ff)
    put(v_ref, fw, 2 * mw, 3 * mw)
    cols = wt_ref.shape[1]
    parts = [wt_ref[o_mi:o_fq, :], jnp.zeros((n_rows - 2 * mh - fh, cols), F32), wt_ref[o_ff:o_ff + fh, :],
             jnp.zeros((LANES - n_rows, cols), F32)]
    gate_ref[...] = jnp.concatenate([p for p in parts if p.shape[0]], axis=0).T.astype(BF16)


def _regroup(wt, *, mw, fw, mh, fh, k_scale, n_rows):
    total, d = wt.shape
    cb = _fit(d, TILES["regroup_cols"], LANES)
    kern = functools.partial(_regroup_kernel, mw=mw, fw=fw, mh=mh, fh=fh, k_scale=k_scale, n_rows=n_rows)
    return pl.pallas_call(
        kern,
        grid=(d // cb,),
        in_specs=[pl.BlockSpec((total, cb), lambda i: (0, i))],
        out_specs=[
            pl.BlockSpec((cb, 3 * mw), lambda i: (i, 0)),
            pl.BlockSpec((cb, 2 * fw), lambda i: (i, 0)),
            pl.BlockSpec((cb, fw + mw), lambda i: (i, 0)),
            pl.BlockSpec((cb, LANES), lambda i: (i, 0)),
        ],
        out_shape=[
            jax.ShapeDtypeStruct((d, 3 * mw), BF16),
            jax.ShapeDtypeStruct((d, 2 * fw), BF16),
            jax.ShapeDtypeStruct((d, fw + mw), BF16),
            jax.ShapeDtypeStruct((d, LANES), BF16),
        ],
        compiler_params=_params(("parallel",), 40),
        name="regroup_w_in",
    )(wt)


def _gates_kernel(zg_ref, b_ref, row_ref, kb_ref, ab_ref, carry_scr, *, n_pass, n_rows, n_attn):
    @pl.when(pl.program_id(1) == 0)
    def _():
        carry_scr[...] = jnp.zeros_like(carry_scr)

    v = zg_ref[...] + b_ref[...]
    blk = v.shape[0]
    lane = lax.broadcasted_iota(jnp.int32, v.shape, 1)
    summed = lane >= n_pass
    ls = jnp.where(summed, _log_sigmoid(v), 0.0)

    def split3(t):
        hi = t.astype(BF16)
        r1 = t - hi.astype(F32)
        mid = r1.astype(BF16)
        return hi, mid, (r1 - mid.astype(F32)).astype(BF16)

    tri = (lax.broadcasted_iota(jnp.int32, (blk, blk), 0) >= lax.broadcasted_iota(jnp.int32, (blk, blk), 1)).astype(BF16)
    local = sum(_dot(tri, t) for t in split3(ls))
    cs = local + carry_scr[0:1, :]
    carry_scr[...] = jnp.broadcast_to(cs[blk - 1:blk, :], carry_scr.shape)
    out = jnp.where(summed, cs, v)
    row_ref[...] = out.T[:n_rows, :]

    a0 = n_rows - n_attn
    src = lax.broadcasted_iota(jnp.int32, (LANES, LANES), 0)
    dst = lax.broadcasted_iota(jnp.int32, (LANES, LANES), 1)
    kb = None
    for c, t in enumerate(split3((local[blk - 1:blk, :] - local) * LOG2E)):
        place = ((src >= a0) & (src < a0 + n_attn) & (dst == N_BIAS_TERMS * (src - a0) + c)).astype(BF16)
        kb = _dot(t, place) if kb is None else kb + _dot(t, place)
    kb_ref[...] = kb.astype(BF16)

    ab = None
    for c, t in enumerate(split3(out - pltpu.roll(out, LANES - n_pass, axis=1))):
        place = ((src < n_pass) & (dst == N_BIAS_TERMS * src + c)).astype(BF16)
        ab = _dot(t, place) if ab is None else ab + _dot(t, place)
    ab_ref[...] = ab.astype(BF16)


def _gates(zg, bias, *, batch, seq, blk, n_pass, n_rows, n_attn):
    nb = seq // blk
    kern = functools.partial(_gates_kernel, n_pass=n_pass, n_rows=n_rows, n_attn=n_attn)
    return pl.pallas_call(
        kern,
        grid=(batch, nb),
        in_specs=[
            pl.BlockSpec((blk, LANES), lambda b, s: (b * nb + s, 0)),
            pl.BlockSpec((1, LANES), lambda b, s: (0, 0)),
        ],
        out_specs=[
            pl.BlockSpec((None, n_rows, blk), lambda b, s: (b, 0, s)),
            pl.BlockSpec((blk, LANES), lambda b, s: (b * nb + s, 0)),
            pl.BlockSpec((blk, LANES), lambda b, s: (b * nb + s, 0)),
        ],
        out_shape=[
            jax.ShapeDtypeStruct((batch, n_rows, seq), F32),
            jax.ShapeDtypeStruct((batch * seq, LANES), BF16),
            jax.ShapeDtypeStruct((batch * seq, LANES), BF16),
        ],
        scratch_shapes=[pltpu.VMEM((8, LANES), F32)],
        compiler_params=_params(("arbitrary", "arbitrary"), 32),
        name="gate_prep",
    )(zg, bias)


def _mlstm_kernel(q_ref, k_ref, o_ref, vt_ref, row_ref, ab_ref, og_ref, out_ref, ct_scr, n_scr, m_scr, *, heads, d):
    @pl.when(pl.program_id(1) == 0)
    def _():
        ct_scr[...] = jnp.zeros_like(ct_scr)
        n_scr[...] = jnp.zeros_like(n_scr)
        m_scr[...] = jnp.zeros_like(m_scr)

    L = q_ref.shape[0]
    below = lax.broadcasted_iota(jnp.int32, (L, L), 0) <= lax.broadcasted_iota(jnp.int32, (L, L), 1)
    term = lax.broadcasted_iota(jnp.int32, (LANES, L), 0)
    for hh in range(heads):
        sl = slice(hh * d, (hh + 1) * d)
        q, k, vt = q_ref[:, sl], k_ref[:, sl], vt_ref[sl, :]
        pick = jnp.where((term >= N_BIAS_TERMS * hh) & (term < N_BIAS_TERMS * (hh + 1)), 1.0, 0.0).astype(BF16)
        a_bc = _dot(ab_ref[...], pick)
        g_row = row_ref[heads + hh:heads + hh + 1, :]
        m_old = m_scr[hh, 0:1, 0:1]

        mm = jnp.maximum(m_old, jnp.max(jnp.where(below, a_bc, -jnp.inf), axis=0, keepdims=True))
        w = jnp.exp(jnp.where(below, a_bc - mm, -jnp.inf))
        st = _dot_nt(k, q) * w
        w_inter = jnp.exp(m_old - mm)
        num = _dot(vt, st.astype(BF16)) + w_inter * _dot_nt(ct_scr[hh].astype(BF16), q)
        nq = _dot_nt(jnp.broadcast_to(n_scr[hh], (8, d)).astype(BF16), q)[0:1, :]
        den = jnp.sum(st, axis=0, keepdims=True) + w_inter * nq
        hout = num * (1.0 / jnp.maximum(jnp.abs(den), jnp.exp(-(g_row + mm))))
        inv_rms = lax.rsqrt(jnp.mean(hout * hout, axis=0, keepdims=True) + EPS)
        gain = jnp.concatenate([og_ref[sl, :]] * (L // LANES), axis=1)
        gate = jax.nn.sigmoid(o_ref[:, sl].astype(F32))
        out_ref[:, sl] = ((hout * inv_rms * gain).T * gate).astype(BF16)

        m_new = mm[:, L - 1:L]
        decay = jnp.exp(m_old - m_new)
        kw = k.astype(F32) * jnp.exp(a_bc[:, :d] - m_new)
        ct_scr[hh] = decay * ct_scr[hh] + _dot(vt, kw.astype(BF16))
        n_scr[hh] = decay * n_scr[hh] + jnp.sum(kw, axis=0, keepdims=True)
        m_scr[hh] = jnp.broadcast_to(m_new, m_scr.shape[1:])


def _mlstm(z, vt, rows, ab, out_g, *, batch, seq, heads, d, vt_block):
    L = _fit(seq, TILES["mlstm_chunk"], LANES)
    assert L == d
    nc = seq // L
    width = heads * d
    n_rows = rows.shape[1]
    kern = functools.partial(_mlstm_kernel, heads=heads, d=d)
    og_b = jnp.broadcast_to(out_g.reshape(width, 1), (width, LANES))

    def zspec(group):
        return pl.BlockSpec((L, width), lambda b, c: (b * nc + c, group))

    return pl.pallas_call(
        kern,
        grid=(batch, nc),
        in_specs=[
            zspec(0), zspec(1), zspec(2),
            pl.BlockSpec((None, width, L), lambda b, c: (b, vt_block, c)),
            pl.BlockSpec((None, n_rows, L), lambda b, c: (b, 0, c)),
            pl.BlockSpec((L, LANES), lambda b, c: (b * nc + c, 0)),
            pl.BlockSpec((width, LANES), lambda b, c: (0, 0)),
        ],
        out_specs=pl.BlockSpec((L, width), lambda b, c: (b * nc + c, 0)),
        out_shape=jax.ShapeDtypeStruct((batch * seq, width), BF16),
        scratch_shapes=[pltpu.VMEM((heads, d, d), F32), pltpu.VMEM((heads, 1, d), F32), pltpu.VMEM((heads, 8, LANES), F32)],
        compiler_params=_params(("parallel", "arbitrary"), 32),
        name="mlstm",
    )(z, z, z, vt, rows, ab, og_b)


def _fox_kernel(fend_ref, fq_ref, win_ref, q_ref, k_ref, kb_ref, vt_ref, f_ref, o_ref, qa_scr, s0_scr, s1_scr, acc_scr, *, tq, tk, heads):
    b, h, i = pl.program_id(0), pl.program_id(1), pl.program_id(2)
    n_diag = tq // tk
    q0 = pl.multiple_of(i * tq, tq)
    lane = lax.broadcasted_iota(jnp.int32, (tq, LANES), 1)
    qa_scr[:, :LANES] = q_ref[...]
    qa_scr[:, LANES:] = jnp.where((lane >= N_BIAS_TERMS * h) & (lane < N_BIAS_TERMS * (h + 1)), 1.0, 0.0).astype(BF16)
    acc_scr[...] = jnp.zeros_like(acc_scr)

    def f_end(k0):
        return f_ref[:, pl.ds(pl.multiple_of(k0 + tk - LANES, LANES), LANES)][:, LANES - 1:LANES]

    f_ref_q = f_end(q0 + tq - tk)

    def key_block(k0):
        return jnp.concatenate([k_ref[pl.ds(k0, tk), :], kb_ref[pl.ds(k0, tk), :]], axis=1)

    def scores(k0, s_ref):
        s_ref[...] = _dot_nt(key_block(k0), qa_scr[...])

    def update(k0, s, carry, c0):
        m, l = carry
        cols = slice(c0, None) if c0 else slice(None)
        off = (f_ref_q - f_end(k0)) * LOG2E
        m_new = jnp.maximum(m[:, cols], jnp.max(s, axis=0, keepdims=True) + off)
        alpha = jnp.exp2(m[:, cols] - m_new)
        p = jnp.exp2(s - (m_new - off))
        l_new = alpha * l[:, cols] + jnp.sum(p, axis=0, keepdims=True)
        acc_scr[:, cols] = alpha * acc_scr[:, cols] + _dot(vt_ref[:, pl.ds(k0, tk)], p.astype(BF16))
        if c0:
            m_new = jnp.concatenate([m[:, :c0], m_new], axis=1)
            l_new = jnp.concatenate([l[:, :c0], l_new], axis=1)
        return m_new, l_new

    row = b * heads + h
    thr = fq_ref[row, i] + win_ref[0, 0]
    n_skip = lax.fori_loop(0, i * n_diag, lambda j, c: c + jnp.where(fend_ref[row, j] >= thr, 1, 0), jnp.int32(0))
    t0 = n_skip // 2

    scores(pl.multiple_of(t0 * (2 * tk), 2 * tk), s0_scr)

    def pair(t, carry):
        k0 = pl.multiple_of(t * (2 * tk), 2 * tk)
        scores(k0 + tk, s1_scr)
        carry = update(k0, s0_scr[...], carry, 0)
        scores(k0 + 2 * tk, s0_scr)
        return update(k0 + tk, s1_scr[...], carry, 0)

    carry = (jnp.full((1, tq), NEG_BIG, F32), jnp.zeros((1, tq), F32))
    n_pairs = q0 // (2 * tk) - t0
    carry = lax.fori_loop(0, n_pairs // 2, lambda u, c: pair(t0 + 2 * u + 1, pair(t0 + 2 * u, c)), carry)
    carry = lax.fori_loop(t0 + n_pairs // 2 * 2, q0 // (2 * tk), pair, carry)

    slots = (s0_scr, s1_scr)
    tri = lax.broadcasted_iota(jnp.int32, (tk, tk), 0) <= lax.broadcasted_iota(jnp.int32, (tk, tk), 1)
    for jj in range(n_diag):
        if jj + 1 < n_diag:
            scores(q0 + (jj + 1) * tk, slots[(jj + 1) % 2])
        c0 = jj * tk
        s = slots[jj % 2][:, c0:]
        head = jnp.where(tri, s[:, :tk], NEG_BIG)
        s = head if s.shape[1] == tk else jnp.concatenate([head, s[:, tk:]], axis=1)
        carry = update(q0 + c0, s, carry, c0)
    _, l = carry
    o_ref[...] = (acc_scr[...] / l).T.astype(BF16)


def _fox(z, kb, vt, frow, win, *, batch, seq, heads, d, tq, tk, q_group, f_row0):
    assert (tq // tk) % 2 == 0
    nq = seq // tq
    per_group = z.shape[1] // (q_group + 2) // d
    kern = functools.partial(_fox_kernel, tq=tq, tk=tk, heads=heads)
    f4 = frow.reshape(batch, frow.shape[1], 1, seq)
    fh = frow[:, f_row0:f_row0 + heads, :]
    f_end = fh[:, :, tk - 1::tk].reshape(batch * heads, seq // tk)
    f_q = fh[:, :, ::tq].reshape(batch * heads, nq)
    smem = pl.BlockSpec(memory_space=pltpu.SMEM)
    return pl.pallas_call(
        kern,
        grid=(batch, heads, nq),
        in_specs=[
            smem, smem, smem,
            pl.BlockSpec((tq, d), lambda b, h, i: (b * nq + i, q_group * per_group + h)),
            pl.BlockSpec((seq, d), lambda b, h, i: (b, (q_group + 1) * per_group + h)),
            pl.BlockSpec((seq, LANES), lambda b, h, i: (b, 0)),
            pl.BlockSpec((None, d, seq), lambda b, h, i: (b, h, 0)),
            pl.BlockSpec((None, None, 1, seq), lambda b, h, i: (b, f_row0 + h, 0, 0)),
        ],
        out_specs=pl.BlockSpec((tq, d), lambda b, h, i: (b * nq + i, h)),
        out_shape=jax.ShapeDtypeStruct((batch * seq, heads * d), BF16),
        scratch_shapes=[pltpu.VMEM((tq, 2 * LANES), BF16), pltpu.VMEM((tk, tq), F32), pltpu.VMEM((tk, tq), F32), pltpu.VMEM((d, tq), F32)],
        compiler_params=_params(("parallel", "parallel", "arbitrary"), 48),
        name="fox_attention",
    )(f_end, f_q, win, z, z, kb, vt, f4)


def _out_proj_kernel(x_ref, ha_ref, hb_ref, wa_ref, wb_ref, g_ref, x1_ref, h_ref):
    x1 = x_ref[...] + _dot(ha_ref[...], wa_ref[...]) + _dot(hb_ref[...], wb_ref[...])
    x1_ref[...] = x1
    h_ref[...] = _rms(x1, g_ref[...]).astype(BF16)


def _out_proj(x2d, ha, hb, wa, wb, g):
    n, d = x2d.shape
    tm = _fit(n, TILES["out_tm"], BF16_SUBLANES)
    ka, kb = ha.shape[1], hb.shape[1]
    const = dict(pipeline_mode=pl.Buffered(1))
    return pl.pallas_call(
        _out_proj_kernel,
        grid=(n // tm,),
        in_specs=[
            pl.BlockSpec((tm, d), lambda i: (i, 0)),
            pl.BlockSpec((tm, ka), lambda i: (i, 0)),
            pl.BlockSpec((tm, kb), lambda i: (i, 0)),
            pl.BlockSpec((ka, d), lambda i: (0, 0), **const),
            pl.BlockSpec((kb, d), lambda i: (0, 0), **const),
            pl.BlockSpec((1, d), lambda i: (0, 0)),
        ],
        out_specs=[pl.BlockSpec((tm, d), lambda i: (i, 0)), pl.BlockSpec((tm, d), lambda i: (i, 0))],
        out_shape=[jax.ShapeDtypeStruct((n, d), F32), jax.ShapeDtypeStruct((n, d), BF16)],
        compiler_params=_params(("parallel",), 56),
        name="out_proj",
    )(x2d, ha, hb, wa, wb, g)


def _ffn_up_kernel(h_ref, halo_ref, wa_ref, wg_ref, cw_ref, cb_ref, act_ref, a_scr, *, tiles_per_seq, halo):
    i = pl.program_id(0)
    tm = h_ref.shape[0]
    h = h_ref[...]
    a = _dot(h, wa_ref[...])
    a_prev = _dot(halo_ref[...], wa_ref[...])
    a_prev = jnp.where(i % tiles_per_seq == 0, 0.0, a_prev)
    a_scr[0:halo, :] = a_prev
    a_scr[halo:, :] = a
    y = (cw_ref[2:3, :] * a + cw_ref[1:2, :] * a_scr[pl.ds(halo - 1, tm), :]
         + cw_ref[0:1, :] * a_scr[pl.ds(halo - 2, tm), :] + cb_ref[...])
    gelu = 0.5 * y * (1.0 + lax.erf(y * (0.5 ** 0.5)))
    act_ref[...] = (gelu * _dot(h, wg_ref[...])).astype(BF16)


def _ffn_up(h2, w_up, conv_w, conv_b, *, seq):
    n, d = h2.shape
    d_ff = conv_w.shape[1]
    halo = BF16_SUBLANES
    tm = _fit(seq, TILES["up_tm"], halo)
    tn = _fit(d_ff, TILES["up_tn"], LANES)
    nj = d_ff // tn
    kern = functools.partial(_ffn_up_kernel, tiles_per_seq=seq // tm, halo=halo)
    return pl.pallas_call(
        kern,
        grid=(n // tm, nj),
        in_specs=[
            pl.BlockSpec((tm, d), lambda i, j: (i, 0)),
            pl.BlockSpec((halo, d), lambda i, j: (jnp.maximum(i * (tm // halo) - 1, 0), 0)),
            pl.BlockSpec((d, tn), lambda i, j: (0, j)),
            pl.BlockSpec((d, tn), lambda i, j: (0, nj + j)),
            pl.BlockSpec((conv_w.shape[0], tn), lambda i, j: (0, j)),
            pl.BlockSpec((1, tn), lambda i, j: (0, j)),
        ],
        out_specs=pl.BlockSpec((tm, tn), lambda i, j: (i, j)),
        out_shape=jax.ShapeDtypeStruct((n, d_ff), BF16),
        scratch_shapes=[pltpu.VMEM((tm + halo, tn), F32)],
        compiler_params=_params(("parallel", "arbitrary"), 48),
        name="ffn_up",
    )(h2, h2, w_up, w_up, conv_w, conv_b)


def _ffn_down_kernel(act_ref, w_ref, x_ref, o_ref):
    o_ref[...] = x_ref[...] + _dot(act_ref[...], w_ref[...])


def _ffn_down(act, w_down, x1):
    n, d_ff = act.shape
    d = w_down.shape[1]
    tm = _fit(n, TILES["down_tm"], BF16_SUBLANES)
    tn = _fit(d, TILES["down_tn"], LANES)
    return pl.pallas_call(
        _ffn_down_kernel,
        grid=(d // tn, n // tm),
        in_specs=[
            pl.BlockSpec((tm, d_ff), lambda j, i: (i, 0)),
            pl.BlockSpec((d_ff, tn), lambda j, i: (0, j)),
            pl.BlockSpec((tm, tn), lambda j, i: (i, j)),
        ],
        out_specs=pl.BlockSpec((tm, tn), lambda j, i: (i, j)),
        out_shape=jax.ShapeDtypeStruct((n, d), F32),
        compiler_params=_params(("arbitrary", "arbitrary"), 56),
        name="ffn_down",
    )(act, w_down, x1)


def _ple_kernel(x_ref, p_ref, g_ref, wg_ref, wp_ref, o_ref):
    x = x_ref[...]
    gate = jax.nn.sigmoid(_dot(_rms(x, g_ref[...]).astype(BF16), wg_ref[...]))
    o_ref[...] = x + gate * _dot(p_ref[...].astype(BF16), wp_ref[...])


def _ple(x2, p2d, g, w_gate, w_proj):
    n, d = x2.shape
    dp = p2d.shape[1]
    tm = _fit(n, TILES["ple_tm"], 8)
    const = dict(pipeline_mode=pl.Buffered(1))
    return pl.pallas_call(
        _ple_kernel,
        grid=(n // tm,),
        in_specs=[
            pl.BlockSpec((tm, d), lambda i: (i, 0)),
            pl.BlockSpec((tm, dp), lambda i: (i, 0)),
            pl.BlockSpec((1, d), lambda i: (0, 0)),
            pl.BlockSpec((d, d), lambda i: (0, 0), **const),
            pl.BlockSpec((dp, d), lambda i: (0, 0), **const),
        ],
        out_specs=pl.BlockSpec((tm, d), lambda i: (i, 0)),
        out_shape=jax.ShapeDtypeStruct((n, d), F32),
        compiler_params=_params(("parallel",), 56),
        name="ple",
    )(x2, p2d, g, w_gate, w_proj)


def kernel(x, p, mix_norm_g, w_in, mlstm_gate_bias, mlstm_out_g, fox_q_g, fox_k_g, fox_f_bias, w_out, ffn_norm_g, w_up, conv_w, conv_b, w_down, ple_norm_g, w_ple_gate, w_ple_proj):
    batch, seq, d_model = x.shape
    depth = w_in.shape[0]
    mh = mlstm_gate_bias.shape[1] // 2
    mw = mlstm_out_g.shape[1]
    md = mw // mh
    fh = fox_f_bias.shape[1]
    fd = fox_q_g.shape[1]
    fw = fh * fd
    n_gate = 2 * mh + fh
    n_rows = -(-n_gate // 8) * 8
    assert mw == fw and fd == LANES and n_rows <= LANES and N_BIAS_TERMS * fh <= LANES
    assert w_in.shape[2] == 4 * mw + 2 * mh + 3 * fw + fh
    tq = _fit(seq, TILES["fox_tq"], LANES)
    tk = _fit(tq, TILES["fox_tk"], LANES)

    xs = x.reshape(batch * seq, d_model)
    for li in range(depth):
        w_plain, w_qk, w_v, w_gate = _regroup(w_in[li].T, mw=mw, fw=fw, mh=mh, fh=fh, k_scale=md ** -0.5, n_rows=n_rows)
        pad = jnp.zeros((d_model, n_rows - n_gate), F32)
        gate_bias = jnp.concatenate([mlstm_gate_bias[li], pad[0], fox_f_bias[li], jnp.zeros((LANES - n_rows,), F32)])[None, :]
        q_gain = (fox_q_g[li] * (fd ** -0.5 * LOG2E))[None, :]
        k_gain = fox_k_g[li][None, :]

        zm, zg, h1 = _proj_plain(xs, mix_norm_g[li][None, :], w_plain, w_gate, tn=mw)
        zf = _proj_headnorm(h1, w_qk, jnp.stack([q_gain, k_gain]), tn=fw)
        vt = _proj_transposed(h1, w_v, batch=batch)
        rows, kb, ab = _gates(zg, gate_bias, batch=batch, seq=seq, blk=tk, n_pass=mh, n_rows=n_rows, n_attn=fh)
        ha = _mlstm(zm, vt, rows, ab, mlstm_out_g[li], batch=batch, seq=seq, heads=mh, d=md, vt_block=1)
        qk_bound = 1.02 * fd * jnp.max(jnp.abs(q_gain)) * jnp.max(jnp.abs(k_gain))
        win = ((2.0 * qk_bound + F32_MIN_EXP) / LOG2E).reshape(1, 1)
        hb = _fox(zf, kb, vt, rows, win, batch=batch, seq=seq, heads=fh, d=fd, tq=tq, tk=tk, q_group=0, f_row0=n_rows - fh)

        wo = w_out[li].astype(BF16)
        x1, h2 = _out_proj(xs, ha, hb, wo[:mw], wo[mw:], ffn_norm_g[li][None, :])
        act = _ffn_up(h2, w_up[li].astype(BF16), conv_w[li], conv_b[li][None, :], seq=seq)
        x2 = _ffn_down(act, w_down[li].astype(BF16), x1)
        xs = _ple(x2, p[li].reshape(batch * seq, -1), ple_norm_g[li][None, :], w_ple_gate[li].astype(BF16), w_ple_proj[li].astype(BF16))
    return xs.reshape(batch, seq, d_model)
```

```python
import functools
import math

import jax
import jax.numpy as jnp
from jax import lax
from jax.experimental import pallas as pl
from jax.experimental.pallas import tpu as pltpu

F32 = jnp.float32
BF16 = jnp.bfloat16
EPS = 1e-6
LANES = 128
BF16_SUBLANES = 16
NEG_BIG = -1e30
LOG2E = math.log2(math.e)
N_BIAS_TERMS = 3
F32_MIN_EXP = 127.0
MIB = 1024 * 1024

TILES = dict(
    regroup_cols=256, in_tm=1024, mlstm_chunk=256, fox_tq=1024, fox_tk=512,
    out_tm=512, up_tm=1024, up_tn=512, down_tm=512, down_tn=1024, ple_tm=512,
)


def _fit(n, pref, align):
    t = min(n, pref)
    while t > align and n % t:
        t //= 2
    assert n % t == 0 and (t % align == 0 or t == n), (n, pref, align)
    return t


def _params(semantics, vmem_mib):
    return pltpu.CompilerParams(dimension_semantics=semantics, vmem_limit_bytes=vmem_mib * MIB)


def _rms(x, g):
    return x * lax.rsqrt(jnp.mean(x * x, axis=-1, keepdims=True) + EPS) * g


def _dot(a, b):
    return jnp.dot(a, b, preferred_element_type=F32)


def _dot_nt(a, b):
    return lax.dot_general(a, b, (((1,), (1,)), ((), ())), preferred_element_type=F32)


def _dot_tn(a, b):
    return lax.dot_general(a, b, (((0,), (0,)), ((), ())), preferred_element_type=F32)


def _log_sigmoid(v):
    return jnp.minimum(v, 0.0) - jnp.log1p(jnp.exp(-jnp.abs(v)))


def _proj_plain_kernel(x_ref, g_ref, wg_ref, w_ref, z_ref, zg_ref, h_ref, h_scr):
    @pl.when(pl.program_id(1) == 0)
    def _():
        hb = _rms(x_ref[...], g_ref[...]).astype(BF16)
        h_scr[...] = hb
        h_ref[...] = hb
        zg_ref[...] = _dot(hb, wg_ref[...])

    z_ref[...] = _dot(h_scr[...], w_ref[...]).astype(BF16)


def _proj_plain(x2d, g, w, w_gate, *, tn):
    n, d = x2d.shape
    tm = _fit(n, TILES["in_tm"], LANES)
    return pl.pallas_call(
        _proj_plain_kernel,
        grid=(n // tm, w.shape[1] // tn),
        in_specs=[
            pl.BlockSpec((tm, d), lambda i, j: (i, 0)),
            pl.BlockSpec((1, d), lambda i, j: (0, 0)),
            pl.BlockSpec((d, LANES), lambda i, j: (0, 0)),
            pl.BlockSpec((d, tn), lambda i, j: (0, j)),
        ],
        out_specs=[
            pl.BlockSpec((tm, tn), lambda i, j: (i, j)),
            pl.BlockSpec((tm, LANES), lambda i, j: (i, 0)),
            pl.BlockSpec((tm, d), lambda i, j: (i, 0)),
        ],
        out_shape=[
            jax.ShapeDtypeStruct((n, w.shape[1]), BF16),
            jax.ShapeDtypeStruct((n, LANES), F32),
            jax.ShapeDtypeStruct((n, d), BF16),
        ],
        scratch_shapes=[pltpu.VMEM((tm, d), BF16)],
        compiler_params=_params(("parallel", "arbitrary"), 56),
        name="proj_plain",
    )(x2d, g, w_gate, w)


def _proj_headnorm_kernel(h_ref, w_ref, gain_ref, z_ref):
    acc = _dot(h_ref[...], w_ref[...])
    hd = gain_ref.shape[-1]
    for hh in range(acc.shape[1] // hd):
        sl = slice(hh * hd, (hh + 1) * hd)
        z_ref[:, sl] = _rms(acc[:, sl], gain_ref[...]).astype(BF16)


def _proj_headnorm(h, w, gains, *, tn):
    n, d = h.shape
    hd = gains.shape[-1]
    tm = _fit(n, TILES["in_tm"], LANES)
    return pl.pallas_call(
        _proj_headnorm_kernel,
        grid=(n // tm, w.shape[1] // tn),
        in_specs=[
            pl.BlockSpec((tm, d), lambda i, j: (i, 0)),
            pl.BlockSpec((d, tn), lambda i, j: (0, j)),
            pl.BlockSpec((None, 1, hd), lambda i, j: (j, 0, 0)),
        ],
        out_specs=pl.BlockSpec((tm, tn), lambda i, j: (i, j)),
        out_shape=jax.ShapeDtypeStruct((n, w.shape[1]), BF16),
        compiler_params=_params(("parallel", "arbitrary"), 56),
        name="proj_headnorm",
    )(h, w, gains)


def _proj_transposed_kernel(h_ref, w_ref, vt_ref):
    vt_ref[...] = lax.dot_general(w_ref[...], h_ref[...], (((0,), (1,)), ((), ())), preferred_element_type=F32).astype(BF16)


def _proj_transposed(h, w, *, batch):
    n, d = h.shape
    seq = n // batch
    tn = w.shape[1]
    tm = _fit(seq, TILES["in_tm"], LANES)
    tiles_per_seq = seq // tm
    return pl.pallas_call(
        _proj_transposed_kernel,
        grid=(n // tm,),
        in_specs=[
            pl.BlockSpec((tm, d), lambda i: (i, 0)),
            pl.BlockSpec((d, tn), lambda i: (0, 0), pipeline_mode=pl.Buffered(1)),
        ],
        out_specs=pl.BlockSpec((None, tn, tm), lambda i: (i // tiles_per_seq, 0, i % tiles_per_seq)),
        out_shape=jax.ShapeDtypeStruct((batch, tn, seq), BF16),
        compiler_params=_params(("parallel",), 56),
        name="proj_transposed",
    )(h, w)


def _regroup_kernel(wt_ref, plain_ref, qk_ref, v_ref, gate_ref, *, mw, fw, mh, fh, k_scale, n_rows):
    o_mi = 4 * mw
    o_fq = o_mi + 2 * mh
    o_ff = o_fq + 3 * fw

    def put(dst, c0, r0, r1, scale=None):
        t = wt_ref[r0:r1, :].T
        dst[:, c0:c0 + (r1 - r0)] = (t if scale is None else t * scale).astype(BF16)

    put(plain_ref, 0, 0, mw)
    put(plain_ref, mw, mw, 2 * mw, k_scale)
    put(plain_ref, 2 * mw, 3 * mw, o_mi)
    put(qk_ref, 0, o_fq, o_fq + 2 * fw)
    put(v_ref, 0, o_fq + 2 * fw, o_ff)
    put(v_ref, fw, 2 * mw, 3 * mw)
    cols = wt_ref.shape[1]
    parts = [wt_ref[o_mi:o_fq, :], jnp.zeros((n_rows - 2 * mh - fh, cols), F32), wt_ref[o_ff:o_ff + fh, :],
             jnp.zeros((LANES - n_rows, cols), F32)]
    gate_ref[...] = jnp.concatenate([p for p in parts if p.shape[0]], axis=0).T.astype(BF16)


def _regroup(wt, *, mw, fw, mh, fh, k_scale, n_rows):
    total, d = wt.shape
    cb = _fit(d, TILES["regroup_cols"], LANES)
    kern = functools.partial(_regroup_kernel, mw=mw, fw=fw, mh=mh, fh=fh, k_scale=k_scale, n_rows=n_rows)
    return pl.pallas_call(
        kern,
        grid=(d // cb,),
        in_specs=[pl.BlockSpec((total, cb), lambda i: (0, i))],
        out_specs=[
            pl.BlockSpec((cb, 3 * mw), lambda i: (i, 0)),
            pl.BlockSpec((cb, 2 * fw), lambda i: (i, 0)),
            pl.BlockSpec((cb, fw + mw), lambda i: (i, 0)),
            pl.BlockSpec((cb, LANES), lambda i: (i, 0)),
        ],
        out_shape=[
            jax.ShapeDtypeStruct((d, 3 * mw), BF16),
            jax.ShapeDtypeStruct((d, 2 * fw), BF16),
            jax.ShapeDtypeStruct((d, fw + mw), BF16),
            jax.ShapeDtypeStruct((d, LANES), BF16),
        ],
        compiler_params=_params(("parallel",), 40),
        name="regroup_w_in",
    )(wt)


def _gates_kernel(zg_ref, b_ref, row_ref, kb_ref, ab_ref, carry_scr, *, n_pass, n_rows, n_attn):
    @pl.when(pl.program_id(1) == 0)
    def _():
        carry_scr[...] = jnp.zeros_like(carry_scr)

    v = zg_ref[...] + b_ref[...]
    blk = v.shape[0]
    lane = lax.broadcasted_iota(jnp.int32, v.shape, 1)
    summed = lane >= n_pass
    ls = jnp.where(summed, _log_sigmoid(v), 0.0)

    def split3(t):
        hi = t.astype(BF16)
        r1 = t - hi.astype(F32)
        mid = r1.astype(BF16)
        return hi, mid, (r1 - mid.astype(F32)).astype(BF16)

    tri = (lax.broadcasted_iota(jnp.int32, (blk, blk), 0) >= lax.broadcasted_iota(jnp.int32, (blk, blk), 1)).astype(BF16)
    local = sum(_dot(tri, t) for t in split3(ls))
    cs = local + carry_scr[0:1, :]
    carry_scr[...] = jnp.broadcast_to(cs[blk - 1:blk, :], carry_scr.shape)
    out = jnp.where(summed, cs, v)
    row_ref[...] = out.T[:n_rows, :]

    a0 = n_rows - n_attn
    src = lax.broadcasted_iota(jnp.int32, (LANES, LANES), 0)
    dst = lax.broadcasted_iota(jnp.int32, (LANES, LANES), 1)
    kb = None
    for c, t in enumerate(split3((local[blk - 1:blk, :] - local) * LOG2E)):
        place = ((src >= a0) & (src < a0 + n_attn) & (dst == N_BIAS_TERMS * (src - a0) + c)).astype(BF16)
        kb = _dot(t, place) if kb is None else kb + _dot(t, place)
    kb_ref[...] = kb.astype(BF16)

    ab = None
    for c, t in enumerate(split3(out - pltpu.roll(out, LANES - n_pass, axis=1))):
        place = ((src < n_pass) & (dst == N_BIAS_TERMS * src + c)).astype(BF16)
        ab = _dot(t, place) if ab is None else ab + _dot(t, place)
    ab_ref[...] = ab.astype(BF16)


def _gates(zg, bias, *, batch, seq, blk, n_pass, n_rows, n_attn):
    nb = seq // blk
    kern = functools.partial(_gates_kernel, n_pass=n_pass, n_rows=n_rows, n_attn=n_attn)
    return pl.pallas_call(
        kern,
        grid=(batch, nb),
        in_specs=[
            pl.BlockSpec((blk, LANES), lambda b, s: (b * nb + s, 0)),
            pl.BlockSpec((1, LANES), lambda b, s: (0, 0)),
        ],
        out_specs=[
            pl.BlockSpec((None, n_rows, blk), lambda b, s: (b, 0, s)),
            pl.BlockSpec((blk, LANES), lambda b, s: (b * nb + s, 0)),
            pl.BlockSpec((blk, LANES), lambda b, s: (b * nb + s, 0)),
        ],
        out_shape=[
            jax.ShapeDtypeStruct((batch, n_rows, seq), F32),
            jax.ShapeDtypeStruct((batch * seq, LANES), BF16),
            jax.ShapeDtypeStruct((batch * seq, LANES), BF16),
        ],
        scratch_shapes=[pltpu.VMEM((8, LANES), F32)],
        compiler_params=_params(("arbitrary", "arbitrary"), 32),
        name="gate_prep",
    )(zg, bias)


def _mlstm_kernel(q_ref, k_ref, o_ref, vt_ref, row_ref, ab_ref, og_ref, out_ref, ct_scr, n_scr, m_scr, *, heads, d):
    @pl.when(pl.program_id(1) == 0)
    def _():
        ct_scr[...] = jnp.zeros_like(ct_scr)
        n_scr[...] = jnp.zeros_like(n_scr)
        m_scr[...] = jnp.zeros_like(m_scr)

    L = q_ref.shape[0]
    below = lax.broadcasted_iota(jnp.int32, (L, L), 0) <= lax.broadcasted_iota(jnp.int32, (L, L), 1)
    term = lax.broadcasted_iota(jnp.int32, (LANES, L), 0)
    for hh in range(heads):
        sl = slice(hh * d, (hh + 1) * d)
        q, k, vt = q_ref[:, sl], k_ref[:, sl], vt_ref[sl, :]
        pick = jnp.where((term >= N_BIAS_TERMS * hh) & (term < N_BIAS_TERMS * (hh + 1)), 1.0, 0.0).astype(BF16)
        a_bc = _dot(ab_ref[...], pick)
        g_row = row_ref[heads + hh:heads + hh + 1, :]
        m_old = m_scr[hh, 0:1, 0:1]

        mm = jnp.maximum(m_old, jnp.max(jnp.where(below, a_bc, -jnp.inf), axis=0, keepdims=True))
        w = jnp.exp(jnp.where(below, a_bc - mm, -jnp.inf))
        st = _dot_nt(k, q) * w
        w_inter = jnp.exp(m_old - mm)
        num = _dot(vt, st.astype(BF16)) + w_inter * _dot_nt(ct_scr[hh].astype(BF16), q)
        nq = _dot_nt(jnp.broadcast_to(n_scr[hh], (8, d)).astype(BF16), q)[0:1, :]
        den = jnp.sum(st, axis=0, keepdims=True) + w_inter * nq
        hout = num * (1.0 / jnp.maximum(jnp.abs(den), jnp.exp(-(g_row + mm))))
        inv_rms = lax.rsqrt(jnp.mean(hout * hout, axis=0, keepdims=True) + EPS)
        gain = jnp.concatenate([og_ref[sl, :]] * (L // LANES), axis=1)
        gate = jax.nn.sigmoid(o_ref[:, sl].astype(F32))
        out_ref[:, sl] = ((hout * inv_rms * gain).T * gate).astype(BF16)

        m_new = mm[:, L - 1:L]
        decay = jnp.exp(m_old - m_new)
        kw = k.astype(F32) * jnp.exp(a_bc[:, :d] - m_new)
        ct_scr[hh] = decay * ct_scr[hh] + _dot(vt, kw.astype(BF16))
        n_scr[hh] = decay * n_scr[hh] + jnp.sum(kw, axis=0, keepdims=True)
        m_scr[hh] = jnp.broadcast_to(m_new, m_scr.shape[1:])


def _mlstm(z, vt, rows, ab, out_g, *, batch, seq, heads, d, vt_block):
    L = _fit(seq, TILES["mlstm_chunk"], LANES)
    assert L == d
    nc = seq // L
    width = heads * d
    n_rows = rows.shape[1]
    kern = functools.partial(_mlstm_kernel, heads=heads, d=d)
    og_b = jnp.broadcast_to(out_g.reshape(width, 1), (width, LANES))

    def zspec(group):
        return pl.BlockSpec((L, width), lambda b, c: (b * nc + c, group))

    return pl.pallas_call(
        kern,
        grid=(batch, nc),
        in_specs=[
            zspec(0), zspec(1), zspec(2),
            pl.BlockSpec((None, width, L), lambda b, c: (b, vt_block, c)),
            pl.BlockSpec((None, n_rows, L), lambda b, c: (b, 0, c)),
            pl.BlockSpec((L, LANES), lambda b, c: (b * nc + c, 0)),
            pl.BlockSpec((width, LANES), lambda b, c: (0, 0)),
        ],
        out_specs=pl.BlockSpec((L, width), lambda b, c: (b * nc + c, 0)),
        out_shape=jax.ShapeDtypeStruct((batch * seq, width), BF16),
        scratch_shapes=[pltpu.VMEM((heads, d, d), F32), pltpu.VMEM((heads, 1, d), F32), pltpu.VMEM((heads, 8, LANES), F32)],
        compiler_params=_params(("parallel", "arbitrary"), 32),
        name="mlstm",
    )(z, z, z, vt, rows, ab, og_b)


def _fox_kernel(fend_ref, fq_ref, win_ref, q_ref, k_ref, kb_ref, vt_ref, f_ref, o_ref, qa_scr, s0_scr, s1_scr, acc_scr, *, tq, tk, heads):
    b, h, i = pl.program_id(0), pl.program_id(1), pl.program_id(2)
    n_diag = tq // tk
    q0 = pl.multiple_of(i * tq, tq)
    lane = lax.broadcasted_iota(jnp.int32, (tq, LANES), 1)
    qa_scr[:, :LANES] = q_ref[...]
    qa_scr[:, LANES:] = jnp.where((lane >= N_BIAS_TERMS * h) & (lane < N_BIAS_TERMS * (h + 1)), 1.0, 0.0).astype(BF16)
    acc_scr[...] = jnp.zeros_like(acc_scr)

    def f_end(k0):
        return f_ref[:, pl.ds(pl.multiple_of(k0 + tk - LANES, LANES), LANES)][:, LANES - 1:LANES]

    f_ref_q = f_end(q0 + tq - tk)

    def key_block(k0):
        return jnp.concatenate([k_ref[pl.ds(k0, tk), :], kb_ref[pl.ds(k0, tk), :]], axis=1)

    def scores(k0, s_ref):
        s_ref[...] = _dot_nt(key_block(k0), qa_scr[...])

    def update(k0, s, carry, c0):
        m, l = carry
        cols = slice(c0, None) if c0 else slice(None)
        off = (f_ref_q - f_end(k0)) * LOG2E
        m_new = jnp.maximum(m[:, cols], jnp.max(s, axis=0, keepdims=True) + off)
        alpha = jnp.exp2(m[:, cols] - m_new)
        p = jnp.exp2(s - (m_new - off))
        l_new = alpha * l[:, cols] + jnp.sum(p, axis=0, keepdims=True)
        acc_scr[:, cols] = alpha * acc_scr[:, cols] + _dot(vt_ref[:, pl.ds(k0, tk)], p.astype(BF16))
        if c0:
            m_new = jnp.concatenate([m[:, :c0], m_new], axis=1)
            l_new = jnp.concatenate([l[:, :c0], l_new], axis=1)
        return m_new, l_new

    row = b * heads + h
    thr = fq_ref[row, i] + win_ref[0, 0]
    n_skip = lax.fori_loop(0, i * n_diag, lambda j, c: c + jnp.where(fend_ref[row, j] >= thr, 1, 0), jnp.int32(0))
    t0 = n_skip // 2

    scores(pl.multiple_of(t0 * (2 * tk), 2 * tk), s0_scr)

    def pair(t, carry):
        k0 = pl.multiple_of(t * (2 * tk), 2 * tk)
        scores(k0 + tk, s1_scr)
        carry = update(k0, s0_scr[...], carry, 0)
        scores(k0 + 2 * tk, s0_scr)
        return update(k0 + tk, s1_scr[...], carry, 0)

    carry = (jnp.full((1, tq), NEG_BIG, F32), jnp.zeros((1, tq), F32))
    n_pairs = q0 // (2 * tk) - t0
    carry = lax.fori_loop(0, n_pairs // 2, lambda u, c: pair(t0 + 2 * u + 1, pair(t0 + 2 * u, c)), carry)
    carry = lax.fori_loop(t0 + n_pairs // 2 * 2, q0 // (2 * tk), pair, carry)

    slots = (s0_scr, s1_scr)
    tri = lax.broadcasted_iota(jnp.int32, (tk, tk), 0) <= lax.broadcasted_iota(jnp.int32, (tk, tk), 1)
    for jj in range(n_diag):
        if jj + 1 < n_diag:
            c1 = (jj + 1) * tk
            slots[(jj + 1) % 2][:, c1:] = _dot_nt(key_block(q0 + c1), qa_scr[c1:, :])
        c0 = jj * tk
        s = slots[jj % 2][:, c0:]
        head = jnp.where(tri, s[:, :tk], NEG_BIG)
        s = head if s.shape[1] == tk else jnp.concatenate([head, s[:, tk:]], axis=1)
        carry = update(q0 + c0, s, carry, c0)
    _, l = carry
    o_ref[...] = (acc_scr[...] / l).T.astype(BF16)


def _fox(z, kb, vt, frow, win, *, batch, seq, heads, d, tq, tk, q_group, f_row0):
    assert (tq // tk) % 2 == 0
    nq = seq // tq
    per_group = z.shape[1] // (q_group + 2) // d
    kern = functools.partial(_fox_kernel, tq=tq, tk=tk, heads=heads)
    f4 = frow.reshape(batch, frow.shape[1], 1, seq)
    fh = frow[:, f_row0:f_row0 + heads, :]
    f_end = fh[:, :, tk - 1::tk].reshape(batch * heads, seq // tk)
    f_q = fh[:, :, ::tq].reshape(batch * heads, nq)
    smem = pl.BlockSpec(memory_space=pltpu.SMEM)
    return pl.pallas_call(
        kern,
        grid=(batch, heads, nq),
        in_specs=[
            smem, smem, smem,
            pl.BlockSpec((tq, d), lambda b, h, i: (b * nq + i, q_group * per_group + h)),
            pl.BlockSpec((seq, d), lambda b, h, i: (b, (q_group + 1) * per_group + h)),
            pl.BlockSpec((seq, LANES), lambda b, h, i: (b, 0)),
            pl.BlockSpec((None, d, seq), lambda b, h, i: (b, h, 0)),
            pl.BlockSpec((None, None, 1, seq), lambda b, h, i: (b, f_row0 + h, 0, 0)),
        ],
        out_specs=pl.BlockSpec((tq, d), lambda b, h, i: (b * nq + i, h)),
        out_shape=jax.ShapeDtypeStruct((batch * seq, heads * d), BF16),
        scratch_shapes=[pltpu.VMEM((tq, 2 * LANES), BF16), pltpu.VMEM((tk, tq), F32), pltpu.VMEM((tk, tq), F32), pltpu.VMEM((d, tq), F32)],
        compiler_params=_params(("parallel", "parallel", "arbitrary"), 48),
        name="fox_attention",
    )(f_end, f_q, win, z, z, kb, vt, f4)


def _out_proj_kernel(x_ref, ha_ref, hb_ref, wa_ref, wb_ref, g_ref, x1_ref, h_ref):
    x1 = x_ref[...] + _dot(ha_ref[...], wa_ref[...]) + _dot(hb_ref[...], wb_ref[...])
    x1_ref[...] = x1
    h_ref[...] = _rms(x1, g_ref[...]).astype(BF16)


def _out_proj(x2d, ha, hb, wa, wb, g):
    n, d = x2d.shape
    tm = _fit(n, TILES["out_tm"], BF16_SUBLANES)
    ka, kb = ha.shape[1], hb.shape[1]
    const = dict(pipeline_mode=pl.Buffered(1))
    return pl.pallas_call(
        _out_proj_kernel,
        grid=(n // tm,),
        in_specs=[
            pl.BlockSpec((tm, d), lambda i: (i, 0)),
            pl.BlockSpec((tm, ka), lambda i: (i, 0)),
            pl.BlockSpec((tm, kb), lambda i: (i, 0)),
            pl.BlockSpec((ka, d), lambda i: (0, 0), **const),
            pl.BlockSpec((kb, d), lambda i: (0, 0), **const),
            pl.BlockSpec((1, d), lambda i: (0, 0)),
        ],
        out_specs=[pl.BlockSpec((tm, d), lambda i: (i, 0)), pl.BlockSpec((tm, d), lambda i: (i, 0))],
        out_shape=[jax.ShapeDtypeStruct((n, d), F32), jax.ShapeDtypeStruct((n, d), BF16)],
        compiler_params=_params(("parallel",), 56),
        name="out_proj",
    )(x2d, ha, hb, wa, wb, g)


def _ffn_up_kernel(h_ref, halo_ref, wa_ref, wg_ref, cw_ref, cb_ref, act_ref, wa_scr, wg_scr, a_scr, *, tiles_per_seq, halo):
    i = pl.program_id(1)

    @pl.when(i == 0)
    def _():
        wa_scr[...] = wa_ref[...].astype(BF16)
        wg_scr[...] = wg_ref[...].astype(BF16)

    tm = h_ref.shape[0]
    h = h_ref[...]
    a = _dot(h, wa_scr[...])
    a_prev = _dot(halo_ref[...], wa_scr[...])
    a_prev = jnp.where(i % tiles_per_seq == 0, 0.0, a_prev)
    a_scr[0:halo, :] = a_prev
    a_scr[halo:, :] = a
    y = (cw_ref[2:3, :] * a + cw_ref[1:2, :] * a_scr[pl.ds(halo - 1, tm), :]
         + cw_ref[0:1, :] * a_scr[pl.ds(halo - 2, tm), :] + cb_ref[...])
    gelu = 0.5 * y * (1.0 + lax.erf(y * (0.5 ** 0.5)))
    act_ref[...] = (gelu * _dot(h, wg_scr[...])).astype(BF16)


def _ffn_up(h2, w_up, conv_w, conv_b, *, seq):
    n, d = h2.shape
    d_ff = conv_w.shape[1]
    halo = BF16_SUBLANES
    tm = _fit(seq, TILES["up_tm"], halo)
    tn = _fit(d_ff, TILES["up_tn"], LANES)
    nj = d_ff // tn
    kern = functools.partial(_ffn_up_kernel, tiles_per_seq=seq // tm, halo=halo)
    return pl.pallas_call(
        kern,
        grid=(nj, n // tm),
        in_specs=[
            pl.BlockSpec((tm, d), lambda j, i: (i, 0)),
            pl.BlockSpec((halo, d), lambda j, i: (jnp.maximum(i * (tm // halo) - 1, 0), 0)),
            pl.BlockSpec((d, tn), lambda j, i: (0, j)),
            pl.BlockSpec((d, tn), lambda j, i: (0, nj + j)),
            pl.BlockSpec((conv_w.shape[0], tn), lambda j, i: (0, j)),
            pl.BlockSpec((1, tn), lambda j, i: (0, j)),
        ],
        out_specs=pl.BlockSpec((tm, tn), lambda j, i: (i, j)),
        out_shape=jax.ShapeDtypeStruct((n, d_ff), BF16),
        scratch_shapes=[pltpu.VMEM((d, tn), BF16), pltpu.VMEM((d, tn), BF16), pltpu.VMEM((tm + halo, tn), F32)],
        compiler_params=_params(("arbitrary", "arbitrary"), 56),
        name="ffn_up",
    )(h2, h2, w_up, w_up, conv_w, conv_b)


def _ffn_down_kernel(act_ref, w_ref, x_ref, o_ref):
    o_ref[...] = x_ref[...] + _dot(act_ref[...], w_ref[...])


def _ffn_down(act, w_down, x1):
    n, d_ff = act.shape
    d = w_down.shape[1]
    tm = _fit(n, TILES["down_tm"], BF16_SUBLANES)
    tn = _fit(d, TILES["down_tn"], LANES)
    return pl.pallas_call(
        _ffn_down_kernel,
        grid=(d // tn, n // tm),
        in_specs=[
            pl.BlockSpec((tm, d_ff), lambda j, i: (i, 0)),
            pl.BlockSpec((d_ff, tn), lambda j, i: (0, j)),
            pl.BlockSpec((tm, tn), lambda j, i: (i, j)),
        ],
        out_specs=pl.BlockSpec((tm, tn), lambda j, i: (i, j)),
        out_shape=jax.ShapeDtypeStruct((n, d), F32),
        compiler_params=_params(("arbitrary", "arbitrary"), 56),
        name="ffn_down",
    )(act, w_down, x1)


def _ple_kernel(x_ref, p_ref, g_ref, wg_ref, wp_ref, o_ref):
    x = x_ref[...]
    gate = jax.nn.sigmoid(_dot(_rms(x, g_ref[...]).astype(BF16), wg_ref[...]))
    o_ref[...] = x + gate * _dot(p_ref[...].astype(BF16), wp_ref[...])


def _ple(x2, p2d, g, w_gate, w_proj):
    n, d = x2.shape
    dp = p2d.shape[1]
    tm = _fit(n, TILES["ple_tm"], 8)
    const = dict(pipeline_mode=pl.Buffered(1))
    return pl.pallas_call(
        _ple_kernel,
        grid=(n // tm,),
        in_specs=[
            pl.BlockSpec((tm, d), lambda i: (i, 0)),
            pl.BlockSpec((tm, dp), lambda i: (i, 0)),
            pl.BlockSpec((1, d), lambda i: (0, 0)),
            pl.BlockSpec((d, d), lambda i: (0, 0), **const),
            pl.BlockSpec((dp, d), lambda i: (0, 0), **const),
        ],
        out_specs=pl.BlockSpec((tm, d), lambda i: (i, 0)),
        out_shape=jax.ShapeDtypeStruct((n, d), F32),
        compiler_params=_params(("parallel",), 56),
        name="ple",
    )(x2, p2d, g, w_gate, w_proj)


def kernel(x, p, mix_norm_g, w_in, mlstm_gate_bias, mlstm_out_g, fox_q_g, fox_k_g, fox_f_bias, w_out, ffn_norm_g, w_up, conv_w, conv_b, w_down, ple_norm_g, w_ple_gate, w_ple_proj):
    batch, seq, d_model = x.shape
    depth = w_in.shape[0]
    mh = mlstm_gate_bias.shape[1] // 2
    mw = mlstm_out_g.shape[1]
    md = mw // mh
    fh = fox_f_bias.shape[1]
    fd = fox_q_g.shape[1]
    fw = fh * fd
    n_gate = 2 * mh + fh
    n_rows = -(-n_gate // 8) * 8
    assert mw == fw and fd == LANES and n_rows <= LANES and N_BIAS_TERMS * fh <= LANES
    assert w_in.shape[2] == 4 * mw + 2 * mh + 3 * fw + fh
    tq = _fit(seq, TILES["fox_tq"], LANES)
    tk = _fit(tq, TILES["fox_tk"], LANES)

    xs = x.reshape(batch * seq, d_model)
    for li in range(depth):
        w_plain, w_qk, w_v, w_gate = _regroup(w_in[li].T, mw=mw, fw=fw, mh=mh, fh=fh, k_scale=md ** -0.5, n_rows=n_rows)
        pad = jnp.zeros((d_model, n_rows - n_gate), F32)
        gate_bias = jnp.concatenate([mlstm_gate_bias[li], pad[0], fox_f_bias[li], jnp.zeros((LANES - n_rows,), F32)])[None, :]
        q_gain = (fox_q_g[li] * (fd ** -0.5 * LOG2E))[None, :]
        k_gain = fox_k_g[li][None, :]

        zm, zg, h1 = _proj_plain(xs, mix_norm_g[li][None, :], w_plain, w_gate, tn=mw)
        zf = _proj_headnorm(h1, w_qk, jnp.stack([q_gain, k_gain]), tn=fw)
        vt = _proj_transposed(h1, w_v, batch=batch)
        rows, kb, ab = _gates(zg, gate_bias, batch=batch, seq=seq, blk=tk, n_pass=mh, n_rows=n_rows, n_attn=fh)
        ha = _mlstm(zm, vt, rows, ab, mlstm_out_g[li], batch=batch, seq=seq, heads=mh, d=md, vt_block=1)
        qk_bound = 1.02 * fd * jnp.max(jnp.abs(q_gain)) * jnp.max(jnp.abs(k_gain))
        win = ((2.0 * qk_bound + F32_MIN_EXP) / LOG2E).reshape(1, 1)
        hb = _fox(zf, kb, vt, rows, win, batch=batch, seq=seq, heads=fh, d=fd, tq=tq, tk=tk, q_group=0, f_row0=n_rows - fh)

        wo = w_out[li].astype(BF16)
        x1, h2 = _out_proj(xs, ha, hb, wo[:mw], wo[mw:], ffn_norm_g[li][None, :])
        act = _ffn_up(h2, w_up[li], conv_w[li], conv_b[li][None, :], seq=seq)
        x2 = _ffn_down(act, w_down[li].astype(BF16), x1)
        xs = _ple(x2, p[li].reshape(batch * seq, -1), ple_norm_g[li][None, :], w_ple_gate[li].astype(BF16), w_ple_proj[li].astype(BF16))
    return xs.reshape(batch, seq, d_model)
```

```python
import functools
import math

import jax
import jax.numpy as jnp
from jax import lax
from jax.experimental import pallas as pl
from jax.experimental.pallas import tpu as pltpu

F32 = jnp.float32
BF16 = jnp.bfloat16
EPS = 1e-6
LANES = 128
BF16_SUBLANES = 16
NEG_BIG = -1e30
LOG2E = math.log2(math.e)
N_BIAS_TERMS = 3
F32_MIN_EXP = 127.0
MIB = 1024 * 1024

TILES = dict(
    regroup_cols=256, in_tm=1024, mlstm_chunk=256, fox_tq=1024, fox_tk=512,
    out_tm=512, up_tm=1024, up_tn=512, down_tm=512, down_tn=1024, ple_tm=512,
)


def _fit(n, pref, align):
    t = min(n, pref)
    while t > align and n % t:
        t //= 2
    assert n % t == 0 and (t % align == 0 or t == n), (n, pref, align)
    return t


def _params(semantics, vmem_mib):
    return pltpu.CompilerParams(dimension_semantics=semantics, vmem_limit_bytes=vmem_mib * MIB)


def _rms(x, g):
    return x * lax.rsqrt(jnp.mean(x * x, axis=-1, keepdims=True) + EPS) * g


def _dot(a, b):
    return jnp.dot(a, b, preferred_element_type=F32)


def _dot_nt(a, b):
    return lax.dot_general(a, b, (((1,), (1,)), ((), ())), preferred_element_type=F32)


def _dot_tn(a, b):
    return lax.dot_general(a, b, (((0,), (0,)), ((), ())), preferred_element_type=F32)


def _log_sigmoid(v):
    return jnp.minimum(v, 0.0) - jnp.log1p(jnp.exp(-jnp.abs(v)))


def _proj_plain_kernel(x_ref, g_ref, wg_ref, w_ref, z_ref, zg_ref, h_ref, h_scr):
    @pl.when(pl.program_id(1) == 0)
    def _():
        hb = _rms(x_ref[...], g_ref[...]).astype(BF16)
        h_scr[...] = hb
        h_ref[...] = hb
        zg_ref[...] = _dot(hb, wg_ref[...])

    z_ref[...] = _dot(h_scr[...], w_ref[...]).astype(BF16)


def _proj_plain(x2d, g, w, w_gate, *, tn):
    n, d = x2d.shape
    tm = _fit(n, TILES["in_tm"], LANES)
    return pl.pallas_call(
        _proj_plain_kernel,
        grid=(n // tm, w.shape[1] // tn),
        in_specs=[
            pl.BlockSpec((tm, d), lambda i, j: (i, 0)),
            pl.BlockSpec((1, d), lambda i, j: (0, 0)),
            pl.BlockSpec((d, LANES), lambda i, j: (0, 0)),
            pl.BlockSpec((d, tn), lambda i, j: (0, j)),
        ],
        out_specs=[
            pl.BlockSpec((tm, tn), lambda i, j: (i, j)),
            pl.BlockSpec((tm, LANES), lambda i, j: (i, 0)),
            pl.BlockSpec((tm, d), lambda i, j: (i, 0)),
        ],
        out_shape=[
            jax.ShapeDtypeStruct((n, w.shape[1]), BF16),
            jax.ShapeDtypeStruct((n, LANES), F32),
            jax.ShapeDtypeStruct((n, d), BF16),
        ],
        scratch_shapes=[pltpu.VMEM((tm, d), BF16)],
        compiler_params=_params(("parallel", "arbitrary"), 56),
        name="proj_plain",
    )(x2d, g, w_gate, w)


def _proj_headnorm_kernel(h_ref, w_ref, gain_ref, z_ref):
    acc = _dot(h_ref[...], w_ref[...])
    hd = gain_ref.shape[-1]
    for hh in range(acc.shape[1] // hd):
        sl = slice(hh * hd, (hh + 1) * hd)
        z_ref[:, sl] = _rms(acc[:, sl], gain_ref[...]).astype(BF16)


def _proj_headnorm(h, w, gains, *, tn):
    n, d = h.shape
    hd = gains.shape[-1]
    tm = _fit(n, TILES["in_tm"], LANES)
    return pl.pallas_call(
        _proj_headnorm_kernel,
        grid=(n // tm, w.shape[1] // tn),
        in_specs=[
            pl.BlockSpec((tm, d), lambda i, j: (i, 0)),
            pl.BlockSpec((d, tn), lambda i, j: (0, j)),
            pl.BlockSpec((None, 1, hd), lambda i, j: (j, 0, 0)),
        ],
        out_specs=pl.BlockSpec((tm, tn), lambda i, j: (i, j)),
        out_shape=jax.ShapeDtypeStruct((n, w.shape[1]), BF16),
        compiler_params=_params(("parallel", "arbitrary"), 56),
        name="proj_headnorm",
    )(h, w, gains)


def _proj_transposed_kernel(h_ref, w_ref, vt_ref):
    vt_ref[...] = lax.dot_general(w_ref[...], h_ref[...], (((0,), (1,)), ((), ())), preferred_element_type=F32).astype(BF16)


def _proj_transposed(h, w, *, batch):
    n, d = h.shape
    seq = n // batch
    tn = w.shape[1]
    tm = _fit(seq, TILES["in_tm"], LANES)
    tiles_per_seq = seq // tm
    return pl.pallas_call(
        _proj_transposed_kernel,
        grid=(n // tm,),
        in_specs=[
            pl.BlockSpec((tm, d), lambda i: (i, 0)),
            pl.BlockSpec((d, tn), lambda i: (0, 0), pipeline_mode=pl.Buffered(1)),
        ],
        out_specs=pl.BlockSpec((None, tn, tm), lambda i: (i // tiles_per_seq, 0, i % tiles_per_seq)),
        out_shape=jax.ShapeDtypeStruct((batch, tn, seq), BF16),
        compiler_params=_params(("parallel",), 56),
        name="proj_transposed",
    )(h, w)


def _regroup_kernel(wt_ref, plain_ref, qk_ref, v_ref, gate_ref, *, mw, fw, mh, fh, k_scale, n_rows):
    o_mi = 4 * mw
    o_fq = o_mi + 2 * mh
    o_ff = o_fq + 3 * fw

    def put(dst, c0, r0, r1, scale=None):
        t = wt_ref[r0:r1, :].T
        dst[:, c0:c0 + (r1 - r0)] = (t if scale is None else t * scale).astype(BF16)

    put(plain_ref, 0, 0, mw)
    put(plain_ref, mw, mw, 2 * mw, k_scale)
    put(plain_ref, 2 * mw, 3 * mw, o_mi)
    put(qk_ref, 0, o_fq, o_fq + 2 * fw)
    put(v_ref, 0, o_fq + 2 * fw, o_ff)
    put(v_ref, fw, 2 * mw, 3 * mw)
    cols = wt_ref.shape[1]
    parts = [wt_ref[o_mi:o_fq, :], jnp.zeros((n_rows - 2 * mh - fh, cols), F32), wt_ref[o_ff:o_ff + fh, :],
             jnp.zeros((LANES - n_rows, cols), F32)]
    gate_ref[...] = jnp.concatenate([p for p in parts if p.shape[0]], axis=0).T.astype(BF16)


def _regroup(wt, *, mw, fw, mh, fh, k_scale, n_rows):
    total, d = wt.shape
    cb = _fit(d, TILES["regroup_cols"], LANES)
    kern = functools.partial(_regroup_kernel, mw=mw, fw=fw, mh=mh, fh=fh, k_scale=k_scale, n_rows=n_rows)
    return pl.pallas_call(
        kern,
        grid=(d // cb,),
        in_specs=[pl.BlockSpec((total, cb), lambda i: (0, i))],
        out_specs=[
            pl.BlockSpec((cb, 3 * mw), lambda i: (i, 0)),
            pl.BlockSpec((cb, 2 * fw), lambda i: (i, 0)),
            pl.BlockSpec((cb, fw + mw), lambda i: (i, 0)),
            pl.BlockSpec((cb, LANES), lambda i: (i, 0)),
        ],
        out_shape=[
            jax.ShapeDtypeStruct((d, 3 * mw), BF16),
            jax.ShapeDtypeStruct((d, 2 * fw), BF16),
            jax.ShapeDtypeStruct((d, fw + mw), BF16),
            jax.ShapeDtypeStruct((d, LANES), BF16),
        ],
        compiler_params=_params(("parallel",), 40),
        name="regroup_w_in",
    )(wt)


def _gates_kernel(zg_ref, b_ref, row_ref, kb_ref, ab_ref, carry_scr, *, n_pass, n_rows, n_attn):
    @pl.when(pl.program_id(1) == 0)
    def _():
        carry_scr[...] = jnp.zeros_like(carry_scr)

    v = zg_ref[...] + b_ref[...]
    blk = v.shape[0]
    lane = lax.broadcasted_iota(jnp.int32, v.shape, 1)
    summed = lane >= n_pass
    ls = jnp.where(summed, _log_sigmoid(v), 0.0)

    def split3(t):
        hi = t.astype(BF16)
        r1 = t - hi.astype(F32)
        mid = r1.astype(BF16)
        return hi, mid, (r1 - mid.astype(F32)).astype(BF16)

    tri = (lax.broadcasted_iota(jnp.int32, (blk, blk), 0) >= lax.broadcasted_iota(jnp.int32, (blk, blk), 1)).astype(BF16)
    local = sum(_dot(tri, t) for t in split3(ls))
    cs = local + carry_scr[0:1, :]
    carry_scr[...] = jnp.broadcast_to(cs[blk - 1:blk, :], carry_scr.shape)
    out = jnp.where(summed, cs, v)
    row_ref[...] = out.T[:n_rows, :]

    a0 = n_rows - n_attn
    src = lax.broadcasted_iota(jnp.int32, (LANES, LANES), 0)
    dst = lax.broadcasted_iota(jnp.int32, (LANES, LANES), 1)
    kb = None
    for c, t in enumerate(split3((local[blk - 1:blk, :] - local) * LOG2E)):
        place = ((src >= a0) & (src < a0 + n_attn) & (dst == N_BIAS_TERMS * (src - a0) + c)).astype(BF16)
        kb = _dot(t, place) if kb is None else kb + _dot(t, place)
    kb_ref[...] = kb.astype(BF16)

    ab = None
    for c, t in enumerate(split3(out - pltpu.roll(out, LANES - n_pass, axis=1))):
        place = ((src < n_pass) & (dst == N_BIAS_TERMS * src + c)).astype(BF16)
        ab = _dot(t, place) if ab is None else ab + _dot(t, place)
    ab_ref[...] = ab.astype(BF16)


def _gates(zg, bias, *, batch, seq, blk, n_pass, n_rows, n_attn):
    nb = seq // blk
    kern = functools.partial(_gates_kernel, n_pass=n_pass, n_rows=n_rows, n_attn=n_attn)
    return pl.pallas_call(
        kern,
        grid=(batch, nb),
        in_specs=[
            pl.BlockSpec((blk, LANES), lambda b, s: (b * nb + s, 0)),
            pl.BlockSpec((1, LANES), lambda b, s: (0, 0)),
        ],
        out_specs=[
            pl.BlockSpec((None, n_rows, blk), lambda b, s: (b, 0, s)),
            pl.BlockSpec((blk, LANES), lambda b, s: (b * nb + s, 0)),
            pl.BlockSpec((blk, LANES), lambda b, s: (b * nb + s, 0)),
        ],
        out_shape=[
            jax.ShapeDtypeStruct((batch, n_rows, seq), F32),
            jax.ShapeDtypeStruct((batch * seq, LANES), BF16),
            jax.ShapeDtypeStruct((batch * seq, LANES), BF16),
        ],
        scratch_shapes=[pltpu.VMEM((8, LANES), F32)],
        compiler_params=_params(("arbitrary", "arbitrary"), 32),
        name="gate_prep",
    )(zg, bias)


def _mlstm_kernel(q_ref, k_ref, o_ref, vt_ref, row_ref, ab_ref, og_ref, out_ref, ct_scr, n_scr, m_scr, *, heads, d):
    @pl.when(pl.program_id(1) == 0)
    def _():
        ct_scr[...] = jnp.zeros_like(ct_scr)
        n_scr[...] = jnp.zeros_like(n_scr)
        m_scr[...] = jnp.zeros_like(m_scr)

    L = q_ref.shape[0]
    below = lax.broadcasted_iota(jnp.int32, (L, L), 0) <= lax.broadcasted_iota(jnp.int32, (L, L), 1)
    term = lax.broadcasted_iota(jnp.int32, (LANES, L), 0)
    for hh in range(heads):
        sl = slice(hh * d, (hh + 1) * d)
        q, k, vt = q_ref[:, sl], k_ref[:, sl], vt_ref[sl, :]
        pick = jnp.where((term >= N_BIAS_TERMS * hh) & (term < N_BIAS_TERMS * (hh + 1)), 1.0, 0.0).astype(BF16)
        a_bc = _dot(ab_ref[...], pick)
        g_row = row_ref[heads + hh:heads + hh + 1, :]
        m_old = m_scr[hh, 0:1, 0:1]

        mm = jnp.maximum(m_old, jnp.max(jnp.where(below, a_bc, -jnp.inf), axis=0, keepdims=True))
        w = jnp.exp(jnp.where(below, a_bc - mm, -jnp.inf))
        st = _dot_nt(k, q) * w
        w_inter = jnp.exp(m_old - mm)
        num = _dot(vt, st.astype(BF16)) + w_inter * _dot_nt(ct_scr[hh].astype(BF16), q)
        nq = _dot_nt(jnp.broadcast_to(n_scr[hh], (8, d)).astype(BF16), q)[0:1, :]
        den = jnp.sum(st, axis=0, keepdims=True) + w_inter * nq
        hout = num * (1.0 / jnp.maximum(jnp.abs(den), jnp.exp(-(g_row + mm))))
        inv_rms = lax.rsqrt(jnp.mean(hout * hout, axis=0, keepdims=True) + EPS)
        gain = jnp.concatenate([og_ref[sl, :]] * (L // LANES), axis=1)
        gate = jax.nn.sigmoid(o_ref[:, sl].astype(F32))
        out_ref[:, sl] = ((hout * inv_rms * gain).T * gate).astype(BF16)

        m_new = mm[:, L - 1:L]
        decay = jnp.exp(m_old - m_new)
        kw = k.astype(F32) * jnp.exp(a_bc[:, :d] - m_new)
        ct_scr[hh] = decay * ct_scr[hh] + _dot(vt, kw.astype(BF16))
        n_scr[hh] = decay * n_scr[hh] + jnp.sum(kw, axis=0, keepdims=True)
        m_scr[hh] = jnp.broadcast_to(m_new, m_scr.shape[1:])


def _mlstm(z, vt, rows, ab, out_g, *, batch, seq, heads, d, vt_block):
    L = _fit(seq, TILES["mlstm_chunk"], LANES)
    assert L == d
    nc = seq // L
    width = heads * d
    n_rows = rows.shape[1]
    kern = functools.partial(_mlstm_kernel, heads=heads, d=d)
    og_b = jnp.broadcast_to(out_g.reshape(width, 1), (width, LANES))

    def zspec(group):
        return pl.BlockSpec((L, width), lambda b, c: (b * nc + c, group))

    return pl.pallas_call(
        kern,
        grid=(batch, nc),
        in_specs=[
            zspec(0), zspec(1), zspec(2),
            pl.BlockSpec((None, width, L), lambda b, c: (b, vt_block, c)),
            pl.BlockSpec((None, n_rows, L), lambda b, c: (b, 0, c)),
            pl.BlockSpec((L, LANES), lambda b, c: (b * nc + c, 0)),
            pl.BlockSpec((width, LANES), lambda b, c: (0, 0)),
        ],
        out_specs=pl.BlockSpec((L, width), lambda b, c: (b * nc + c, 0)),
        out_shape=jax.ShapeDtypeStruct((batch * seq, width), BF16),
        scratch_shapes=[pltpu.VMEM((heads, d, d), F32), pltpu.VMEM((heads, 1, d), F32), pltpu.VMEM((heads, 8, LANES), F32)],
        compiler_params=_params(("parallel", "arbitrary"), 32),
        name="mlstm",
    )(z, z, z, vt, rows, ab, og_b)


def _fox_kernel(fend_ref, fq_ref, win_ref, q_ref, k_ref, kb_ref, vt_ref, f_ref, o_ref, qa_scr, s0_scr, s1_scr, acc_scr, *, tq, tk, heads):
    b, h, i = pl.program_id(0), pl.program_id(1), pl.program_id(2)
    n_diag = tq // tk
    q0 = pl.multiple_of(i * tq, tq)
    lane = lax.broadcasted_iota(jnp.int32, (tq, LANES), 1)
    qa_scr[:, :LANES] = q_ref[...]
    qa_scr[:, LANES:] = jnp.where((lane >= N_BIAS_TERMS * h) & (lane < N_BIAS_TERMS * (h + 1)), 1.0, 0.0).astype(BF16)
    acc_scr[...] = jnp.zeros_like(acc_scr)

    def f_end(k0):
        return f_ref[:, pl.ds(pl.multiple_of(k0 + tk - LANES, LANES), LANES)][:, LANES - 1:LANES]

    f_ref_q = f_end(q0 + tq - tk)

    def key_block(k0):
        return jnp.concatenate([k_ref[pl.ds(k0, tk), :], kb_ref[pl.ds(k0, tk), :]], axis=1)

    def scores(k0, s_ref):
        s_ref[...] = _dot_nt(key_block(k0), qa_scr[...])

    def update(k0, s, carry, c0):
        m, l = carry
        cols = slice(c0, None) if c0 else slice(None)
        off = (f_ref_q - f_end(k0)) * LOG2E
        m_new = jnp.maximum(m[:, cols], jnp.max(s, axis=0, keepdims=True) + off)
        alpha = jnp.exp2(m[:, cols] - m_new)
        p = jnp.exp2(s - (m_new - off))
        l_new = alpha * l[:, cols] + jnp.sum(p, axis=0, keepdims=True)
        acc_scr[:, cols] = alpha * acc_scr[:, cols] + _dot(vt_ref[:, pl.ds(k0, tk)], p.astype(BF16))
        if c0:
            m_new = jnp.concatenate([m[:, :c0], m_new], axis=1)
            l_new = jnp.concatenate([l[:, :c0], l_new], axis=1)
        return m_new, l_new

    row = b * heads + h
    thr = fq_ref[row, i] + win_ref[0, 0]
    n_skip = lax.fori_loop(0, i * n_diag, lambda j, c: c + jnp.where(fend_ref[row, j] >= thr, 1, 0), jnp.int32(0))
    n_proc = i * n_diag - n_skip
    lead = n_proc % 2
    b_first = n_skip + lead

    def block_start(blk):
        return pl.multiple_of(blk * tk, tk)

    scores(block_start(b_first), s0_scr)

    def single(_, carry):
        scores(block_start(n_skip), s1_scr)
        return update(block_start(n_skip), s1_scr[...], carry, 0)

    def pair(blk, carry):
        k0 = block_start(blk)
        scores(k0 + tk, s1_scr)
        carry = update(k0, s0_scr[...], carry, 0)
        scores(k0 + 2 * tk, s0_scr)
        return update(k0 + tk, s1_scr[...], carry, 0)

    carry = (jnp.full((1, tq), NEG_BIG, F32), jnp.zeros((1, tq), F32))
    carry = lax.fori_loop(0, lead, single, carry)
    n_pairs = n_proc // 2
    carry = lax.fori_loop(0, n_pairs // 2, lambda u, c: pair(b_first + 4 * u + 2, pair(b_first + 4 * u, c)), carry)
    carry = lax.fori_loop(n_pairs // 2 * 2, n_pairs, lambda u, c: pair(b_first + 2 * u, c), carry)

    slots = (s0_scr, s1_scr)
    tri = lax.broadcasted_iota(jnp.int32, (tk, tk), 0) <= lax.broadcasted_iota(jnp.int32, (tk, tk), 1)
    for jj in range(n_diag):
        if jj + 1 < n_diag:
            c1 = (jj + 1) * tk
            slots[(jj + 1) % 2][:, c1:] = _dot_nt(key_block(q0 + c1), qa_scr[c1:, :])
        c0 = jj * tk
        s = slots[jj % 2][:, c0:]
        head = jnp.where(tri, s[:, :tk], NEG_BIG)
        s = head if s.shape[1] == tk else jnp.concatenate([head, s[:, tk:]], axis=1)
        carry = update(q0 + c0, s, carry, c0)
    _, l = carry
    o_ref[...] = (acc_scr[...] / l).T.astype(BF16)


def _fox(z, kb, vt, frow, win, *, batch, seq, heads, d, tq, tk, q_group, f_row0):
    assert (tq // tk) % 2 == 0
    nq = seq // tq
    per_group = z.shape[1] // (q_group + 2) // d
    kern = functools.partial(_fox_kernel, tq=tq, tk=tk, heads=heads)
    f4 = frow.reshape(batch, frow.shape[1], 1, seq)
    fh = frow[:, f_row0:f_row0 + heads, :]
    f_end = fh[:, :, tk - 1::tk].reshape(batch * heads, seq // tk)
    f_q = fh[:, :, ::tq].reshape(batch * heads, nq)
    smem = pl.BlockSpec(memory_space=pltpu.SMEM)
    return pl.pallas_call(
        kern,
        grid=(batch, heads, nq),
        in_specs=[
            smem, smem, smem,
            pl.BlockSpec((tq, d), lambda b, h, i: (b * nq + i, q_group * per_group + h)),
            pl.BlockSpec((seq, d), lambda b, h, i: (b, (q_group + 1) * per_group + h)),
            pl.BlockSpec((seq, LANES), lambda b, h, i: (b, 0)),
            pl.BlockSpec((None, d, seq), lambda b, h, i: (b, h, 0)),
            pl.BlockSpec((None, None, 1, seq), lambda b, h, i: (b, f_row0 + h, 0, 0)),
        ],
        out_specs=pl.BlockSpec((tq, d), lambda b, h, i: (b * nq + i, h)),
        out_shape=jax.ShapeDtypeStruct((batch * seq, heads * d), BF16),
        scratch_shapes=[pltpu.VMEM((tq, 2 * LANES), BF16), pltpu.VMEM((tk, tq), F32), pltpu.VMEM((tk, tq), F32), pltpu.VMEM((d, tq), F32)],
        compiler_params=_params(("parallel", "parallel", "arbitrary"), 48),
        name="fox_attention",
    )(f_end, f_q, win, z, z, kb, vt, f4)


def _out_proj_kernel(x_ref, ha_ref, hb_ref, wa_ref, wb_ref, g_ref, x1_ref, h_ref):
    x1 = x_ref[...] + _dot(ha_ref[...], wa_ref[...]) + _dot(hb_ref[...], wb_ref[...])
    x1_ref[...] = x1
    h_ref[...] = _rms(x1, g_ref[...]).astype(BF16)


def _out_proj(x2d, ha, hb, wa, wb, g):
    n, d = x2d.shape
    tm = _fit(n, TILES["out_tm"], BF16_SUBLANES)
    ka, kb = ha.shape[1], hb.shape[1]
    const = dict(pipeline_mode=pl.Buffered(1))
    return pl.pallas_call(
        _out_proj_kernel,
        grid=(n // tm,),
        in_specs=[
            pl.BlockSpec((tm, d), lambda i: (i, 0)),
            pl.BlockSpec((tm, ka), lambda i: (i, 0)),
            pl.BlockSpec((tm, kb), lambda i: (i, 0)),
            pl.BlockSpec((ka, d), lambda i: (0, 0), **const),
            pl.BlockSpec((kb, d), lambda i: (0, 0), **const),
            pl.BlockSpec((1, d), lambda i: (0, 0)),
        ],
        out_specs=[pl.BlockSpec((tm, d), lambda i: (i, 0)), pl.BlockSpec((tm, d), lambda i: (i, 0))],
        out_shape=[jax.ShapeDtypeStruct((n, d), F32), jax.ShapeDtypeStruct((n, d), BF16)],
        compiler_params=_params(("parallel",), 56),
        name="out_proj",
    )(x2d, ha, hb, wa, wb, g)


def _ffn_up_kernel(h_ref, halo_ref, wa_ref, wg_ref, cw_ref, cb_ref, act_ref, wa_scr, wg_scr, a_scr, *, tiles_per_seq, halo):
    i = pl.program_id(1)

    @pl.when(i == 0)
    def _():
        wa_scr[...] = wa_ref[...].astype(BF16)
        wg_scr[...] = wg_ref[...].astype(BF16)

    tm = h_ref.shape[0]
    h = h_ref[...]
    a = _dot(h, wa_scr[...])
    a_prev = _dot(halo_ref[...], wa_scr[...])
    a_prev = jnp.where(i % tiles_per_seq == 0, 0.0, a_prev)
    a_scr[0:halo, :] = a_prev
    a_scr[halo:, :] = a
    y = (cw_ref[2:3, :] * a + cw_ref[1:2, :] * a_scr[pl.ds(halo - 1, tm), :]
         + cw_ref[0:1, :] * a_scr[pl.ds(halo - 2, tm), :] + cb_ref[...])
    gelu = 0.5 * y * (1.0 + lax.erf(y * (0.5 ** 0.5)))
    act_ref[...] = (gelu * _dot(h, wg_scr[...])).astype(BF16)


def _ffn_up(h2, w_up, conv_w, conv_b, *, seq):
    n, d = h2.shape
    d_ff = conv_w.shape[1]
    halo = BF16_SUBLANES
    tm = _fit(seq, TILES["up_tm"], halo)
    tn = _fit(d_ff, TILES["up_tn"], LANES)
    nj = d_ff // tn
    kern = functools.partial(_ffn_up_kernel, tiles_per_seq=seq // tm, halo=halo)
    return pl.pallas_call(
        kern,
        grid=(nj, n // tm),
        in_specs=[
            pl.BlockSpec((tm, d), lambda j, i: (i, 0)),
            pl.BlockSpec((halo, d), lambda j, i: (jnp.maximum(i * (tm // halo) - 1, 0), 0)),
            pl.BlockSpec((d, tn), lambda j, i: (0, j)),
            pl.BlockSpec((d, tn), lambda j, i: (0, nj + j)),
            pl.BlockSpec((conv_w.shape[0], tn), lambda j, i: (0, j)),
            pl.BlockSpec((1, tn), lambda j, i: (0, j)),
        ],
        out_specs=pl.BlockSpec((tm, tn), lambda j, i: (i, j)),
        out_shape=jax.ShapeDtypeStruct((n, d_ff), BF16),
        scratch_shapes=[pltpu.VMEM((d, tn), BF16), pltpu.VMEM((d, tn), BF16), pltpu.VMEM((tm + halo, tn), F32)],
        compiler_params=_params(("arbitrary", "arbitrary"), 56),
        name="ffn_up",
    )(h2, h2, w_up, w_up, conv_w, conv_b)


def _ffn_down_kernel(act_ref, w_ref, x_ref, o_ref):
    o_ref[...] = x_ref[...] + _dot(act_ref[...], w_ref[...])


def _ffn_down(act, w_down, x1):
    n, d_ff = act.shape
    d = w_down.shape[1]
    tm = _fit(n, TILES["down_tm"], BF16_SUBLANES)
    tn = _fit(d, TILES["down_tn"], LANES)
    return pl.pallas_call(
        _ffn_down_kernel,
        grid=(d // tn, n // tm),
        in_specs=[
            pl.BlockSpec((tm, d_ff), lambda j, i: (i, 0)),
            pl.BlockSpec((d_ff, tn), lambda j, i: (0, j)),
            pl.BlockSpec((tm, tn), lambda j, i: (i, j)),
        ],
        out_specs=pl.BlockSpec((tm, tn), lambda j, i: (i, j)),
        out_shape=jax.ShapeDtypeStruct((n, d), F32),
        compiler_params=_params(("arbitrary", "arbitrary"), 56),
        name="ffn_down",
    )(act, w_down, x1)


def _ple_kernel(x_ref, p_ref, g_ref, wg_ref, wp_ref, o_ref):
    x = x_ref[...]
    gate = jax.nn.sigmoid(_dot(_rms(x, g_ref[...]).astype(BF16), wg_ref[...]))
    o_ref[...] = x + gate * _dot(p_ref[...].astype(BF16), wp_ref[...])


def _ple(x2, p2d, g, w_gate, w_proj):
    n, d = x2.shape
    dp = p2d.shape[1]
    tm = _fit(n, TILES["ple_tm"], 8)
    const = dict(pipeline_mode=pl.Buffered(1))
    return pl.pallas_call(
        _ple_kernel,
        grid=(n // tm,),
        in_specs=[
            pl.BlockSpec((tm, d), lambda i: (i, 0)),
            pl.BlockSpec((tm, dp), lambda i: (i, 0)),
            pl.BlockSpec((1, d), lambda i: (0, 0)),
            pl.BlockSpec((d, d), lambda i: (0, 0), **const),
            pl.BlockSpec((dp, d), lambda i: (0, 0), **const),
        ],
        out_specs=pl.BlockSpec((tm, d), lambda i: (i, 0)),
        out_shape=jax.ShapeDtypeStruct((n, d), F32),
        compiler_params=_params(("parallel",), 56),
        name="ple",
    )(x2, p2d, g, w_gate, w_proj)


def kernel(x, p, mix_norm_g, w_in, mlstm_gate_bias, mlstm_out_g, fox_q_g, fox_k_g, fox_f_bias, w_out, ffn_norm_g, w_up, conv_w, conv_b, w_down, ple_norm_g, w_ple_gate, w_ple_proj):
    batch, seq, d_model = x.shape
    depth = w_in.shape[0]
    mh = mlstm_gate_bias.shape[1] // 2
    mw = mlstm_out_g.shape[1]
    md = mw // mh
    fh = fox_f_bias.shape[1]
    fd = fox_q_g.shape[1]
    fw = fh * fd
    n_gate = 2 * mh + fh
    n_rows = -(-n_gate // 8) * 8
    assert mw == fw and fd == LANES and n_rows <= LANES and N_BIAS_TERMS * fh <= LANES
    assert w_in.shape[2] == 4 * mw + 2 * mh + 3 * fw + fh
    tq = _fit(seq, TILES["fox_tq"], LANES)
    tk = _fit(tq, TILES["fox_tk"], LANES)

    xs = x.reshape(batch * seq, d_model)
    for li in range(depth):
        w_plain, w_qk, w_v, w_gate = _regroup(w_in[li].T, mw=mw, fw=fw, mh=mh, fh=fh, k_scale=md ** -0.5, n_rows=n_rows)
        pad = jnp.zeros((d_model, n_rows - n_gate), F32)
        gate_bias = jnp.concatenate([mlstm_gate_bias[li], pad[0], fox_f_bias[li], jnp.zeros((LANES - n_rows,), F32)])[None, :]
        q_gain = (fox_q_g[li] * (fd ** -0.5 * LOG2E))[None, :]
        k_gain = fox_k_g[li][None, :]

        zm, zg, h1 = _proj_plain(xs, mix_norm_g[li][None, :], w_plain, w_gate, tn=mw)
        zf = _proj_headnorm(h1, w_qk, jnp.stack([q_gain, k_gain]), tn=fw)
        vt = _proj_transposed(h1, w_v, batch=batch)
        rows, kb, ab = _gates(zg, gate_bias, batch=batch, seq=seq, blk=tk, n_pass=mh, n_rows=n_rows, n_attn=fh)
        ha = _mlstm(zm, vt, rows, ab, mlstm_out_g[li], batch=batch, seq=seq, heads=mh, d=md, vt_block=1)
        qk_bound = 1.02 * fd * jnp.max(jnp.abs(q_gain)) * jnp.max(jnp.abs(k_gain))
        win = ((2.0 * qk_bound + F32_MIN_EXP) / LOG2E).reshape(1, 1)
        hb = _fox(zf, kb, vt, rows, win, batch=batch, seq=seq, heads=fh, d=fd, tq=tq, tk=tk, q_group=0, f_row0=n_rows - fh)

        wo = w_out[li].astype(BF16)
        x1, h2 = _out_proj(xs, ha, hb, wo[:mw], wo[mw:], ffn_norm_g[li][None, :])
        act = _ffn_up(h2, w_up[li], conv_w[li], conv_b[li][None, :], seq=seq)
        x2 = _ffn_down(act, w_down[li].astype(BF16), x1)
        xs = _ple(x2, p[li].reshape(batch * seq, -1), ple_norm_g[li][None, :], w_ple_gate[li].astype(BF16), w_ple_proj[li].astype(BF16))
    return xs.reshape(batch, seq, d_model)
```

```python
import functools
import math

import jax
import jax.numpy as jnp
from jax import lax
from jax.experimental import pallas as pl
from jax.experimental.pallas import tpu as pltpu

F32 = jnp.float32
BF16 = jnp.bfloat16
EPS = 1e-6
LANES = 128
BF16_SUBLANES = 16
NEG_BIG = -1e30
LOG2E = math.log2(math.e)
N_BIAS_TERMS = 3
F32_MIN_EXP = 127.0
MIB = 1024 * 1024

TILES = dict(
    regroup_cols=256, in_tm=1024, mlstm_chunk=256, fox_tq=1024, fox_tk=512,
    out_tm=512, up_tm=1024, up_tn=512, down_tm=512, down_tn=1024, ple_tm=512,
)


def _fit(n, pref, align):
    t = min(n, pref)
    while t > align and n % t:
        t //= 2
    assert n % t == 0 and (t % align == 0 or t == n), (n, pref, align)
    return t


def _params(semantics, vmem_mib):
    return pltpu.CompilerParams(dimension_semantics=semantics, vmem_limit_bytes=vmem_mib * MIB)


def _rms(x, g):
    return x * lax.rsqrt(jnp.mean(x * x, axis=-1, keepdims=True) + EPS) * g


def _dot(a, b):
    return jnp.dot(a, b, preferred_element_type=F32)


def _dot_nt(a, b):
    return lax.dot_general(a, b, (((1,), (1,)), ((), ())), preferred_element_type=F32)


def _dot_tn(a, b):
    return lax.dot_general(a, b, (((0,), (0,)), ((), ())), preferred_element_type=F32)


def _serpentine(i, j, *, nj):
    return jnp.where(i % 2 == 0, j, nj - 1 - j)


def _log_sigmoid(v):
    return jnp.minimum(v, 0.0) - jnp.log1p(jnp.exp(-jnp.abs(v)))


def _proj_plain_kernel(x_ref, g_ref, wg_ref, w_ref, z_ref, zg_ref, h_ref, h_scr):
    @pl.when(pl.program_id(1) == 0)
    def _():
        hb = _rms(x_ref[...], g_ref[...]).astype(BF16)
        h_scr[...] = hb
        h_ref[...] = hb
        zg_ref[...] = _dot(hb, wg_ref[...])

    z_ref[...] = _dot(h_scr[...], w_ref[...]).astype(BF16)


def _proj_plain(x2d, g, w, w_gate, *, tn):
    n, d = x2d.shape
    tm = _fit(n, TILES["in_tm"], LANES)
    nj = w.shape[1] // tn
    col = functools.partial(_serpentine, nj=nj)
    return pl.pallas_call(
        _proj_plain_kernel,
        grid=(n // tm, nj),
        in_specs=[
            pl.BlockSpec((tm, d), lambda i, j: (i, 0)),
            pl.BlockSpec((1, d), lambda i, j: (0, 0)),
            pl.BlockSpec((d, LANES), lambda i, j: (0, 0)),
            pl.BlockSpec((d, tn), lambda i, j: (0, col(i, j))),
        ],
        out_specs=[
            pl.BlockSpec((tm, tn), lambda i, j: (i, col(i, j))),
            pl.BlockSpec((tm, LANES), lambda i, j: (i, 0)),
            pl.BlockSpec((tm, d), lambda i, j: (i, 0)),
        ],
        out_shape=[
            jax.ShapeDtypeStruct((n, w.shape[1]), BF16),
            jax.ShapeDtypeStruct((n, LANES), F32),
            jax.ShapeDtypeStruct((n, d), BF16),
        ],
        scratch_shapes=[pltpu.VMEM((tm, d), BF16)],
        compiler_params=_params(("parallel", "arbitrary"), 56),
        name="proj_plain",
    )(x2d, g, w_gate, w)


def _proj_headnorm_kernel(h_ref, w_ref, gain_ref, z_ref):
    acc = _dot(h_ref[...], w_ref[...])
    hd = gain_ref.shape[-1]
    for hh in range(acc.shape[1] // hd):
        sl = slice(hh * hd, (hh + 1) * hd)
        z_ref[:, sl] = _rms(acc[:, sl], gain_ref[...]).astype(BF16)


def _proj_headnorm(h, w, gains, *, tn):
    n, d = h.shape
    hd = gains.shape[-1]
    tm = _fit(n, TILES["in_tm"], LANES)
    nj = w.shape[1] // tn
    col = functools.partial(_serpentine, nj=nj)
    return pl.pallas_call(
        _proj_headnorm_kernel,
        grid=(n // tm, nj),
        in_specs=[
            pl.BlockSpec((tm, d), lambda i, j: (i, 0)),
            pl.BlockSpec((d, tn), lambda i, j: (0, col(i, j))),
            pl.BlockSpec((None, 1, hd), lambda i, j: (col(i, j), 0, 0)),
        ],
        out_specs=pl.BlockSpec((tm, tn), lambda i, j: (i, col(i, j))),
        out_shape=jax.ShapeDtypeStruct((n, w.shape[1]), BF16),
        compiler_params=_params(("parallel", "arbitrary"), 56),
        name="proj_headnorm",
    )(h, w, gains)


def _proj_transposed_kernel(h_ref, w_ref, vt_ref):
    vt_ref[...] = lax.dot_general(w_ref[...], h_ref[...], (((0,), (1,)), ((), ())), preferred_element_type=F32).astype(BF16)


def _proj_transposed(h, w, *, batch):
    n, d = h.shape
    seq = n // batch
    tn = w.shape[1]
    tm = _fit(seq, TILES["in_tm"], LANES)
    tiles_per_seq = seq // tm
    return pl.pallas_call(
        _proj_transposed_kernel,
        grid=(n // tm,),
        in_specs=[
            pl.BlockSpec((tm, d), lambda i: (i, 0)),
            pl.BlockSpec((d, tn), lambda i: (0, 0), pipeline_mode=pl.Buffered(1)),
        ],
        out_specs=pl.BlockSpec((None, tn, tm), lambda i: (i // tiles_per_seq, 0, i % tiles_per_seq)),
        out_shape=jax.ShapeDtypeStruct((batch, tn, seq), BF16),
        compiler_params=_params(("parallel",), 56),
        name="proj_transposed",
    )(h, w)


def _regroup_kernel(wt_ref, plain_ref, qk_ref, v_ref, gate_ref, *, mw, fw, mh, fh, k_scale, n_rows):
    o_mi = 4 * mw
    o_fq = o_mi + 2 * mh
    o_ff = o_fq + 3 * fw

    def put(dst, c0, r0, r1, scale=None):
        t = wt_ref[r0:r1, :].T
        dst[:, c0:c0 + (r1 - r0)] = (t if scale is None else t * scale).astype(BF16)

    put(plain_ref, 0, 0, mw)
    put(plain_ref, mw, mw, 2 * mw, k_scale)
    put(plain_ref, 2 * mw, 3 * mw, o_mi)
    put(qk_ref, 0, o_fq, o_fq + 2 * fw)
    put(v_ref, 0, o_fq + 2 * fw, o_ff)
    put(v_ref, fw, 2 * mw, 3 * mw)
    cols = wt_ref.shape[1]
    parts = [wt_ref[o_mi:o_fq, :], jnp.zeros((n_rows - 2 * mh - fh, cols), F32), wt_ref[o_ff:o_ff + fh, :],
             jnp.zeros((LANES - n_rows, cols), F32)]
    gate_ref[...] = jnp.concatenate([p for p in parts if p.shape[0]], axis=0).T.astype(BF16)


def _regroup(wt, *, mw, fw, mh, fh, k_scale, n_rows):
    total, d = wt.shape
    cb = _fit(d, TILES["regroup_cols"], LANES)
    kern = functools.partial(_regroup_kernel, mw=mw, fw=fw, mh=mh, fh=fh, k_scale=k_scale, n_rows=n_rows)
    return pl.pallas_call(
        kern,
        grid=(d // cb,),
        in_specs=[pl.BlockSpec((total, cb), lambda i: (0, i))],
        out_specs=[
            pl.BlockSpec((cb, 3 * mw), lambda i: (i, 0)),
            pl.BlockSpec((cb, 2 * fw), lambda i: (i, 0)),
            pl.BlockSpec((cb, fw + mw), lambda i: (i, 0)),
            pl.BlockSpec((cb, LANES), lambda i: (i, 0)),
        ],
        out_shape=[
            jax.ShapeDtypeStruct((d, 3 * mw), BF16),
            jax.ShapeDtypeStruct((d, 2 * fw), BF16),
            jax.ShapeDtypeStruct((d, fw + mw), BF16),
            jax.ShapeDtypeStruct((d, LANES), BF16),
        ],
        compiler_params=_params(("parallel",), 40),
        name="regroup_w_in",
    )(wt)


def _gates_kernel(zg_ref, b_ref, row_ref, kb_ref, ab_ref, carry_scr, *, n_pass, n_rows, n_attn):
    @pl.when(pl.program_id(1) == 0)
    def _():
        carry_scr[...] = jnp.zeros_like(carry_scr)

    v = zg_ref[...] + b_ref[...]
    blk = v.shape[0]
    lane = lax.broadcasted_iota(jnp.int32, v.shape, 1)
    summed = lane >= n_pass
    ls = jnp.where(summed, _log_sigmoid(v), 0.0)

    def split3(t):
        hi = t.astype(BF16)
        r1 = t - hi.astype(F32)
        mid = r1.astype(BF16)
        return hi, mid, (r1 - mid.astype(F32)).astype(BF16)

    tri = (lax.broadcasted_iota(jnp.int32, (blk, blk), 0) >= lax.broadcasted_iota(jnp.int32, (blk, blk), 1)).astype(BF16)
    local = sum(_dot(tri, t) for t in split3(ls))
    cs = local + carry_scr[0:1, :]
    carry_scr[...] = jnp.broadcast_to(cs[blk - 1:blk, :], carry_scr.shape)
    out = jnp.where(summed, cs, v)
    row_ref[...] = out.T[:n_rows, :]

    a0 = n_rows - n_attn
    src = lax.broadcasted_iota(jnp.int32, (LANES, LANES), 0)
    dst = lax.broadcasted_iota(jnp.int32, (LANES, LANES), 1)
    kb = None
    for c, t in enumerate(split3((local[blk - 1:blk, :] - local) * LOG2E)):
        place = ((src >= a0) & (src < a0 + n_attn) & (dst == N_BIAS_TERMS * (src - a0) + c)).astype(BF16)
        kb = _dot(t, place) if kb is None else kb + _dot(t, place)
    kb_ref[...] = kb.astype(BF16)

    ab = None
    for c, t in enumerate(split3(out - pltpu.roll(out, LANES - n_pass, axis=1))):
        place = ((src < n_pass) & (dst == N_BIAS_TERMS * src + c)).astype(BF16)
        ab = _dot(t, place) if ab is None else ab + _dot(t, place)
    ab_ref[...] = ab.astype(BF16)


def _gates(zg, bias, *, batch, seq, blk, n_pass, n_rows, n_attn):
    nb = seq // blk
    kern = functools.partial(_gates_kernel, n_pass=n_pass, n_rows=n_rows, n_attn=n_attn)
    return pl.pallas_call(
        kern,
        grid=(batch, nb),
        in_specs=[
            pl.BlockSpec((blk, LANES), lambda b, s: (b * nb + s, 0)),
            pl.BlockSpec((1, LANES), lambda b, s: (0, 0)),
        ],
        out_specs=[
            pl.BlockSpec((None, n_rows, blk), lambda b, s: (b, 0, s)),
            pl.BlockSpec((blk, LANES), lambda b, s: (b * nb + s, 0)),
            pl.BlockSpec((blk, LANES), lambda b, s: (b * nb + s, 0)),
        ],
        out_shape=[
            jax.ShapeDtypeStruct((batch, n_rows, seq), F32),
            jax.ShapeDtypeStruct((batch * seq, LANES), BF16),
            jax.ShapeDtypeStruct((batch * seq, LANES), BF16),
        ],
        scratch_shapes=[pltpu.VMEM((8, LANES), F32)],
        compiler_params=_params(("arbitrary", "arbitrary"), 32),
        name="gate_prep",
    )(zg, bias)


def _mlstm_kernel(q_ref, k_ref, o_ref, vt_ref, row_ref, ab_ref, og_ref, out_ref, ct_scr, n_scr, m_scr, *, heads, d):
    @pl.when(pl.program_id(1) == 0)
    def _():
        ct_scr[...] = jnp.zeros_like(ct_scr)
        n_scr[...] = jnp.zeros_like(n_scr)
        m_scr[...] = jnp.zeros_like(m_scr)

    L = q_ref.shape[0]
    below = lax.broadcasted_iota(jnp.int32, (L, L), 0) <= lax.broadcasted_iota(jnp.int32, (L, L), 1)
    term = lax.broadcasted_iota(jnp.int32, (LANES, L), 0)
    for hh in range(heads):
        sl = slice(hh * d, (hh + 1) * d)
        q, k, vt = q_ref[:, sl], k_ref[:, sl], vt_ref[sl, :]
        pick = jnp.where((term >= N_BIAS_TERMS * hh) & (term < N_BIAS_TERMS * (hh + 1)), 1.0, 0.0).astype(BF16)
        a_bc = _dot(ab_ref[...], pick)
        g_row = row_ref[heads + hh:heads + hh + 1, :]
        m_old = m_scr[hh, 0:1, 0:1]

        mm = jnp.maximum(m_old, jnp.max(jnp.where(below, a_bc, -jnp.inf), axis=0, keepdims=True))
        w = jnp.exp(jnp.where(below, a_bc - mm, -jnp.inf))
        st = _dot_nt(k, q) * w
        w_inter = jnp.exp(m_old - mm)
        num = _dot(vt, st.astype(BF16)) + w_inter * _dot_nt(ct_scr[hh].astype(BF16), q)
        nq = _dot_nt(jnp.broadcast_to(n_scr[hh], (8, d)).astype(BF16), q)[0:1, :]
        den = jnp.sum(st, axis=0, keepdims=True) + w_inter * nq
        hout = num * (1.0 / jnp.maximum(jnp.abs(den), jnp.exp(-(g_row + mm))))
        inv_rms = lax.rsqrt(jnp.mean(hout * hout, axis=0, keepdims=True) + EPS)
        gain = jnp.concatenate([og_ref[sl, :]] * (L // LANES), axis=1)
        gate = jax.nn.sigmoid(o_ref[:, sl].astype(F32))
        out_ref[:, sl] = ((hout * inv_rms * gain).T * gate).astype(BF16)

        m_new = mm[:, L - 1:L]
        decay = jnp.exp(m_old - m_new)
        kw = k.astype(F32) * jnp.exp(a_bc[:, :d] - m_new)
        ct_scr[hh] = decay * ct_scr[hh] + _dot(vt, kw.astype(BF16))
        n_scr[hh] = decay * n_scr[hh] + jnp.sum(kw, axis=0, keepdims=True)
        m_scr[hh] = jnp.broadcast_to(m_new, m_scr.shape[1:])


def _mlstm(z, vt, rows, ab, out_g, *, batch, seq, heads, d, vt_block):
    L = _fit(seq, TILES["mlstm_chunk"], LANES)
    assert L == d
    nc = seq // L
    width = heads * d
    n_rows = rows.shape[1]
    kern = functools.partial(_mlstm_kernel, heads=heads, d=d)
    og_b = jnp.broadcast_to(out_g.reshape(width, 1), (width, LANES))

    def zspec(group):
        return pl.BlockSpec((L, width), lambda b, c: (b * nc + c, group))

    return pl.pallas_call(
        kern,
        grid=(batch, nc),
        in_specs=[
            zspec(0), zspec(1), zspec(2),
            pl.BlockSpec((None, width, L), lambda b, c: (b, vt_block, c)),
            pl.BlockSpec((None, n_rows, L), lambda b, c: (b, 0, c)),
            pl.BlockSpec((L, LANES), lambda b, c: (b * nc + c, 0)),
            pl.BlockSpec((width, LANES), lambda b, c: (0, 0)),
        ],
        out_specs=pl.BlockSpec((L, width), lambda b, c: (b * nc + c, 0)),
        out_shape=jax.ShapeDtypeStruct((batch * seq, width), BF16),
        scratch_shapes=[pltpu.VMEM((heads, d, d), F32), pltpu.VMEM((heads, 1, d), F32), pltpu.VMEM((heads, 8, LANES), F32)],
        compiler_params=_params(("parallel", "arbitrary"), 32),
        name="mlstm",
    )(z, z, z, vt, rows, ab, og_b)


def _fox_kernel(fend_ref, fq_ref, win_ref, q_ref, k_ref, kb_ref, vt_ref, f_ref, o_ref, qa_scr, s0_scr, s1_scr, acc_scr, *, tq, tk, heads):
    b, h, i = pl.program_id(0), pl.program_id(1), pl.program_id(2)
    n_diag = tq // tk
    q0 = pl.multiple_of(i * tq, tq)
    lane = lax.broadcasted_iota(jnp.int32, (tq, LANES), 1)
    qa_scr[:, :LANES] = q_ref[...]
    qa_scr[:, LANES:] = jnp.where((lane >= N_BIAS_TERMS * h) & (lane < N_BIAS_TERMS * (h + 1)), 1.0, 0.0).astype(BF16)
    acc_scr[...] = jnp.zeros_like(acc_scr)

    def f_end(k0):
        return f_ref[:, pl.ds(pl.multiple_of(k0 + tk - LANES, LANES), LANES)][:, LANES - 1:LANES]

    f_ref_q = f_end(q0 + tq - tk)

    def key_block(k0):
        return jnp.concatenate([k_ref[pl.ds(k0, tk), :], kb_ref[pl.ds(k0, tk), :]], axis=1)

    def scores(k0, s_ref):
        s_ref[...] = _dot_nt(key_block(k0), qa_scr[...])

    def update(k0, s, carry, c0):
        m, l = carry
        cols = slice(c0, None) if c0 else slice(None)
        off = (f_ref_q - f_end(k0)) * LOG2E
        m_new = jnp.maximum(m[:, cols], jnp.max(s, axis=0, keepdims=True) + off)
        alpha = jnp.exp2(m[:, cols] - m_new)
        p = jnp.exp2(s - (m_new - off))
        l_new = alpha * l[:, cols] + jnp.sum(p, axis=0, keepdims=True)
        acc_scr[:, cols] = alpha * acc_scr[:, cols] + _dot(vt_ref[:, pl.ds(k0, tk)], p.astype(BF16))
        if c0:
            m_new = jnp.concatenate([m[:, :c0], m_new], axis=1)
            l_new = jnp.concatenate([l[:, :c0], l_new], axis=1)
        return m_new, l_new

    row = b * heads + h
    thr = fq_ref[row, i] + win_ref[0, 0]
    n_skip = lax.fori_loop(0, i * n_diag, lambda j, c: c + jnp.where(fend_ref[row, j] >= thr, 1, 0), jnp.int32(0))
    n_proc = i * n_diag - n_skip
    lead = n_proc % 2
    b_first = n_skip + lead

    def block_start(blk):
        return pl.multiple_of(blk * tk, tk)

    scores(block_start(b_first), s0_scr)

    def single(_, carry):
        scores(block_start(n_skip), s1_scr)
        return update(block_start(n_skip), s1_scr[...], carry, 0)

    def pair(blk, carry):
        k0 = block_start(blk)
        scores(k0 + tk, s1_scr)
        carry = update(k0, s0_scr[...], carry, 0)
        scores(k0 + 2 * tk, s0_scr)
        return update(k0 + tk, s1_scr[...], carry, 0)

    carry = (jnp.full((1, tq), NEG_BIG, F32), jnp.zeros((1, tq), F32))
    carry = lax.fori_loop(0, lead, single, carry)
    n_pairs = n_proc // 2
    carry = lax.fori_loop(0, n_pairs // 2, lambda u, c: pair(b_first + 4 * u + 2, pair(b_first + 4 * u, c)), carry)
    carry = lax.fori_loop(n_pairs // 2 * 2, n_pairs, lambda u, c: pair(b_first + 2 * u, c), carry)

    slots = (s0_scr, s1_scr)
    tri = lax.broadcasted_iota(jnp.int32, (tk, tk), 0) <= lax.broadcasted_iota(jnp.int32, (tk, tk), 1)
    for jj in range(n_diag):
        if jj + 1 < n_diag:
            c1 = (jj + 1) * tk
            slots[(jj + 1) % 2][:, c1:] = _dot_nt(key_block(q0 + c1), qa_scr[c1:, :])
        c0 = jj * tk
        s = slots[jj % 2][:, c0:]
        head = jnp.where(tri, s[:, :tk], NEG_BIG)
        s = head if s.shape[1] == tk else jnp.concatenate([head, s[:, tk:]], axis=1)
        carry = update(q0 + c0, s, carry, c0)
    _, l = carry
    o_ref[...] = (acc_scr[...] / l).T.astype(BF16)


def _fox(z, kb, vt, frow, win, *, batch, seq, heads, d, tq, tk, q_group, f_row0):
    assert (tq // tk) % 2 == 0
    nq = seq // tq
    per_group = z.shape[1] // (q_group + 2) // d
    kern = functools.partial(_fox_kernel, tq=tq, tk=tk, heads=heads)
    f4 = frow.reshape(batch, frow.shape[1], 1, seq)
    fh = frow[:, f_row0:f_row0 + heads, :]
    f_end = fh[:, :, tk - 1::tk].reshape(batch * heads, seq // tk)
    f_q = fh[:, :, ::tq].reshape(batch * heads, nq)
    smem = pl.BlockSpec(memory_space=pltpu.SMEM)
    return pl.pallas_call(
        kern,
        grid=(batch, heads, nq),
        in_specs=[
            smem, smem, smem,
            pl.BlockSpec((tq, d), lambda b, h, i: (b * nq + i, q_group * per_group + h)),
            pl.BlockSpec((seq, d), lambda b, h, i: (b, (q_group + 1) * per_group + h)),
            pl.BlockSpec((seq, LANES), lambda b, h, i: (b, 0)),
            pl.BlockSpec((None, d, seq), lambda b, h, i: (b, h, 0)),
            pl.BlockSpec((None, None, 1, seq), lambda b, h, i: (b, f_row0 + h, 0, 0)),
        ],
        out_specs=pl.BlockSpec((tq, d), lambda b, h, i: (b * nq + i, h)),
        out_shape=jax.ShapeDtypeStruct((batch * seq, heads * d), BF16),
        scratch_shapes=[pltpu.VMEM((tq, 2 * LANES), BF16), pltpu.VMEM((tk, tq), F32), pltpu.VMEM((tk, tq), F32), pltpu.VMEM((d, tq), F32)],
        compiler_params=_params(("parallel", "parallel", "arbitrary"), 48),
        name="fox_attention",
    )(f_end, f_q, win, z, z, kb, vt, f4)


def _out_proj_kernel(x_ref, ha_ref, hb_ref, wa_ref, wb_ref, g_ref, x1_ref, h_ref):
    x1 = x_ref[...] + _dot(ha_ref[...], wa_ref[...]) + _dot(hb_ref[...], wb_ref[...])
    x1_ref[...] = x1
    h_ref[...] = _rms(x1, g_ref[...]).astype(BF16)


def _out_proj(x2d, ha, hb, wa, wb, g):
    n, d = x2d.shape
    tm = _fit(n, TILES["out_tm"], BF16_SUBLANES)
    ka, kb = ha.shape[1], hb.shape[1]
    const = dict(pipeline_mode=pl.Buffered(1))
    return pl.pallas_call(
        _out_proj_kernel,
        grid=(n // tm,),
        in_specs=[
            pl.BlockSpec((tm, d), lambda i: (i, 0)),
            pl.BlockSpec((tm, ka), lambda i: (i, 0)),
            pl.BlockSpec((tm, kb), lambda i: (i, 0)),
            pl.BlockSpec((ka, d), lambda i: (0, 0), **const),
            pl.BlockSpec((kb, d), lambda i: (0, 0), **const),
            pl.BlockSpec((1, d), lambda i: (0, 0)),
        ],
        out_specs=[pl.BlockSpec((tm, d), lambda i: (i, 0)), pl.BlockSpec((tm, d), lambda i: (i, 0))],
        out_shape=[jax.ShapeDtypeStruct((n, d), F32), jax.ShapeDtypeStruct((n, d), BF16)],
        compiler_params=_params(("parallel",), 56),
        name="out_proj",
    )(x2d, ha, hb, wa, wb, g)


def _ffn_up_kernel(h_ref, halo_ref, wa_ref, wg_ref, cw_ref, cb_ref, act_ref, wa_scr, wg_scr, a_scr, *, tiles_per_seq, halo):
    i = pl.program_id(1)

    @pl.when(i == 0)
    def _():
        wa_scr[...] = wa_ref[...].astype(BF16)
        wg_scr[...] = wg_ref[...].astype(BF16)

    tm = h_ref.shape[0]
    h = h_ref[...]
    a = _dot(h, wa_scr[...])
    a_prev = _dot(halo_ref[...], wa_scr[...])
    a_prev = jnp.where(i % tiles_per_seq == 0, 0.0, a_prev)
    a_scr[0:halo, :] = a_prev
    a_scr[halo:, :] = a
    y = (cw_ref[2:3, :] * a + cw_ref[1:2, :] * a_scr[pl.ds(halo - 1, tm), :]
         + cw_ref[0:1, :] * a_scr[pl.ds(halo - 2, tm), :] + cb_ref[...])
    gelu = 0.5 * y * (1.0 + lax.erf(y * (0.5 ** 0.5)))
    act_ref[...] = (gelu * _dot(h, wg_scr[...])).astype(BF16)


def _ffn_up(h2, w_up, conv_w, conv_b, *, seq):
    n, d = h2.shape
    d_ff = conv_w.shape[1]
    halo = BF16_SUBLANES
    tm = _fit(seq, TILES["up_tm"], halo)
    tn = _fit(d_ff, TILES["up_tn"], LANES)
    nj = d_ff // tn
    kern = functools.partial(_ffn_up_kernel, tiles_per_seq=seq // tm, halo=halo)
    return pl.pallas_call(
        kern,
        grid=(nj, n // tm),
        in_specs=[
            pl.BlockSpec((tm, d), lambda j, i: (i, 0)),
            pl.BlockSpec((halo, d), lambda j, i: (jnp.maximum(i * (tm // halo) - 1, 0), 0)),
            pl.BlockSpec((d, tn), lambda j, i: (0, j)),
            pl.BlockSpec((d, tn), lambda j, i: (0, nj + j)),
            pl.BlockSpec((conv_w.shape[0], tn), lambda j, i: (0, j)),
            pl.BlockSpec((1, tn), lambda j, i: (0, j)),
        ],
        out_specs=pl.BlockSpec((tm, tn), lambda j, i: (i, j)),
        out_shape=jax.ShapeDtypeStruct((n, d_ff), BF16),
        scratch_shapes=[pltpu.VMEM((d, tn), BF16), pltpu.VMEM((d, tn), BF16), pltpu.VMEM((tm + halo, tn), F32)],
        compiler_params=_params(("arbitrary", "arbitrary"), 56),
        name="ffn_up",
    )(h2, h2, w_up, w_up, conv_w, conv_b)


def _ffn_down_kernel(act_ref, w_ref, x_ref, o_ref):
    o_ref[...] = x_ref[...] + _dot(act_ref[...], w_ref[...])


def _ffn_down(act, w_down, x1):
    n, d_ff = act.shape
    d = w_down.shape[1]
    tm = _fit(n, TILES["down_tm"], BF16_SUBLANES)
    tn = _fit(d, TILES["down_tn"], LANES)
    return pl.pallas_call(
        _ffn_down_kernel,
        grid=(d // tn, n // tm),
        in_specs=[
            pl.BlockSpec((tm, d_ff), lambda j, i: (i, 0)),
            pl.BlockSpec((d_ff, tn), lambda j, i: (0, j)),
            pl.BlockSpec((tm, tn), lambda j, i: (i, j)),
        ],
        out_specs=pl.BlockSpec((tm, tn), lambda j, i: (i, j)),
        out_shape=jax.ShapeDtypeStruct((n, d), F32),
        compiler_params=_params(("arbitrary", "arbitrary"), 56),
        name="ffn_down",
    )(act, w_down, x1)


def _ple_kernel(x_ref, p_ref, g_ref, wg_ref, wp_ref, o_ref):
    x = x_ref[...]
    gate = jax.nn.sigmoid(_dot(_rms(x, g_ref[...]).astype(BF16), wg_ref[...]))
    o_ref[...] = x + gate * _dot(p_ref[...].astype(BF16), wp_ref[...])


def _ple(x2, p2d, g, w_gate, w_proj):
    n, d = x2.shape
    dp = p2d.shape[1]
    tm = _fit(n, TILES["ple_tm"], 8)
    const = dict(pipeline_mode=pl.Buffered(1))
    return pl.pallas_call(
        _ple_kernel,
        grid=(n // tm,),
        in_specs=[
            pl.BlockSpec((tm, d), lambda i: (i, 0)),
            pl.BlockSpec((tm, dp), lambda i: (i, 0)),
            pl.BlockSpec((1, d), lambda i: (0, 0)),
            pl.BlockSpec((d, d), lambda i: (0, 0), **const),
            pl.BlockSpec((dp, d), lambda i: (0, 0), **const),
        ],
        out_specs=pl.BlockSpec((tm, d), lambda i: (i, 0)),
        out_shape=jax.ShapeDtypeStruct((n, d), F32),
        compiler_params=_params(("parallel",), 56),
        name="ple",
    )(x2, p2d, g, w_gate, w_proj)


def kernel(x, p, mix_norm_g, w_in, mlstm_gate_bias, mlstm_out_g, fox_q_g, fox_k_g, fox_f_bias, w_out, ffn_norm_g, w_up, conv_w, conv_b, w_down, ple_norm_g, w_ple_gate, w_ple_proj):
    batch, seq, d_model = x.shape
    depth = w_in.shape[0]
    mh = mlstm_gate_bias.shape[1] // 2
    mw = mlstm_out_g.shape[1]
    md = mw // mh
    fh = fox_f_bias.shape[1]
    fd = fox_q_g.shape[1]
    fw = fh * fd
    n_gate = 2 * mh + fh
    n_rows = -(-n_gate // 8) * 8
    assert mw == fw and fd == LANES and n_rows <= LANES and N_BIAS_TERMS * fh <= LANES
    assert w_in.shape[2] == 4 * mw + 2 * mh + 3 * fw + fh
    tq = _fit(seq, TILES["fox_tq"], LANES)
    tk = _fit(tq, TILES["fox_tk"], LANES)

    xs = x.reshape(batch * seq, d_model)
    for li in range(depth):
        w_plain, w_qk, w_v, w_gate = _regroup(w_in[li].T, mw=mw, fw=fw, mh=mh, fh=fh, k_scale=md ** -0.5, n_rows=n_rows)
        pad = jnp.zeros((d_model, n_rows - n_gate), F32)
        gate_bias = jnp.concatenate([mlstm_gate_bias[li], pad[0], fox_f_bias[li], jnp.zeros((LANES - n_rows,), F32)])[None, :]
        q_gain = (fox_q_g[li] * (fd ** -0.5 * LOG2E))[None, :]
        k_gain = fox_k_g[li][None, :]

        zm, zg, h1 = _proj_plain(xs, mix_norm_g[li][None, :], w_plain, w_gate, tn=mw)
        zf = _proj_headnorm(h1, w_qk, jnp.stack([q_gain, k_gain]), tn=fw)
        vt = _proj_transposed(h1, w_v, batch=batch)
        rows, kb, ab = _gates(zg, gate_bias, batch=batch, seq=seq, blk=tk, n_pass=mh, n_rows=n_rows, n_attn=fh)
        ha = _mlstm(zm, vt, rows, ab, mlstm_out_g[li], batch=batch, seq=seq, heads=mh, d=md, vt_block=1)
        qk_bound = 1.02 * fd * jnp.max(jnp.abs(q_gain)) * jnp.max(jnp.abs(k_gain))
        win = ((2.0 * qk_bound + F32_MIN_EXP) / LOG2E).reshape(1, 1)
        hb = _fox(zf, kb, vt, rows, win, batch=batch, seq=seq, heads=fh, d=fd, tq=tq, tk=tk, q_group=0, f_row0=n_rows - fh)

        wo = w_out[li].astype(BF16)
        x1, h2 = _out_proj(xs, ha, hb, wo[:mw], wo[mw:], ffn_norm_g[li][None, :])
        act = _ffn_up(h2, w_up[li], conv_w[li], conv_b[li][None, :], seq=seq)
        x2 = _ffn_down(act, w_down[li].astype(BF16), x1)
        xs = _ple(x2, p[li].reshape(batch * seq, -1), ple_norm_g[li][None, :], w_ple_gate[li].astype(BF16), w_ple_proj[li].astype(BF16))
    return xs.reshape(batch, seq, d_model)
```

```python
import functools
import math

import jax
import jax.numpy as jnp
from jax import lax
from jax.experimental import pallas as pl
from jax.experimental.pallas import tpu as pltpu

F32 = jnp.float32
BF16 = jnp.bfloat16
EPS = 1e-6
LANES = 128
BF16_SUBLANES = 16
NEG_BIG = -1e30
LOG2E = math.log2(math.e)
N_BIAS_TERMS = 3
F32_MIN_EXP = 127.0
MIB = 1024 * 1024

TILES = dict(
    regroup_cols=256, in_tm=1024, mlstm_chunk=256, fox_tq=1024, fox_tk=512,
    out_tm=512, up_tm=1024, up_tn=512, down_tm=512, down_tn=1024, ple_tm=512,
)


def _fit(n, pref, align):
    t = min(n, pref)
    while t > align and n % t:
        t //= 2
    assert n % t == 0 and (t % align == 0 or t == n), (n, pref, align)
    return t


def _params(semantics, vmem_mib):
    return pltpu.CompilerParams(dimension_semantics=semantics, vmem_limit_bytes=vmem_mib * MIB)


def _rms(x, g):
    return x * lax.rsqrt(jnp.mean(x * x, axis=-1, keepdims=True) + EPS) * g


def _dot(a, b):
    return jnp.dot(a, b, preferred_element_type=F32)


def _dot_nt(a, b):
    return lax.dot_general(a, b, (((1,), (1,)), ((), ())), preferred_element_type=F32)


def _dot_tn(a, b):
    return lax.dot_general(a, b, (((0,), (0,)), ((), ())), preferred_element_type=F32)


def _serpentine(i, j, *, nj):
    return jnp.where(i % 2 == 0, j, nj - 1 - j)


def _log_sigmoid(v):
    return jnp.minimum(v, 0.0) - jnp.log1p(jnp.exp(-jnp.abs(v)))


def _proj_plain_kernel(x_ref, g_ref, wg_ref, w_ref, z_ref, zg_ref, h_ref, h_scr):
    @pl.when(pl.program_id(1) == 0)
    def _():
        hb = _rms(x_ref[...], g_ref[...]).astype(BF16)
        h_scr[...] = hb
        h_ref[...] = hb
        zg_ref[...] = _dot(hb, wg_ref[...])

    z_ref[...] = _dot(h_scr[...], w_ref[...]).astype(BF16)


def _proj_plain(x2d, g, w, w_gate, *, tn):
    n, d = x2d.shape
    tm = _fit(n, TILES["in_tm"], LANES)
    nj = w.shape[1] // tn
    col = functools.partial(_serpentine, nj=nj)
    return pl.pallas_call(
        _proj_plain_kernel,
        grid=(n // tm, nj),
        in_specs=[
            pl.BlockSpec((tm, d), lambda i, j: (i, 0)),
            pl.BlockSpec((1, d), lambda i, j: (0, 0)),
            pl.BlockSpec((d, LANES), lambda i, j: (0, 0)),
            pl.BlockSpec((d, tn), lambda i, j: (0, col(i, j))),
        ],
        out_specs=[
            pl.BlockSpec((tm, tn), lambda i, j: (i, col(i, j))),
            pl.BlockSpec((tm, LANES), lambda i, j: (i, 0)),
            pl.BlockSpec((tm, d), lambda i, j: (i, 0)),
        ],
        out_shape=[
            jax.ShapeDtypeStruct((n, w.shape[1]), BF16),
            jax.ShapeDtypeStruct((n, LANES), F32),
            jax.ShapeDtypeStruct((n, d), BF16),
        ],
        scratch_shapes=[pltpu.VMEM((tm, d), BF16)],
        compiler_params=_params(("parallel", "arbitrary"), 56),
        name="proj_plain",
    )(x2d, g, w_gate, w)


def _proj_headnorm_kernel(h_ref, w_ref, gain_ref, z_ref):
    acc = _dot(h_ref[...], w_ref[...])
    hd = gain_ref.shape[-1]
    for hh in range(acc.shape[1] // hd):
        sl = slice(hh * hd, (hh + 1) * hd)
        z_ref[:, sl] = _rms(acc[:, sl], gain_ref[...]).astype(BF16)


def _proj_headnorm(h, w, gains, *, tn):
    n, d = h.shape
    hd = gains.shape[-1]
    tm = _fit(n, TILES["in_tm"], LANES)
    nj = w.shape[1] // tn
    col = functools.partial(_serpentine, nj=nj)
    return pl.pallas_call(
        _proj_headnorm_kernel,
        grid=(n // tm, nj),
        in_specs=[
            pl.BlockSpec((tm, d), lambda i, j: (i, 0)),
            pl.BlockSpec((d, tn), lambda i, j: (0, col(i, j))),
            pl.BlockSpec((None, 1, hd), lambda i, j: (col(i, j), 0, 0)),
        ],
        out_specs=pl.BlockSpec((tm, tn), lambda i, j: (i, col(i, j))),
        out_shape=jax.ShapeDtypeStruct((n, w.shape[1]), BF16),
        compiler_params=_params(("parallel", "arbitrary"), 56),
        name="proj_headnorm",
    )(h, w, gains)


def _proj_transposed_kernel(h_ref, w_ref, vt_ref):
    vt_ref[...] = lax.dot_general(w_ref[...], h_ref[...], (((0,), (1,)), ((), ())), preferred_element_type=F32).astype(BF16)


def _proj_transposed(h, w, *, batch):
    n, d = h.shape
    seq = n // batch
    tn = w.shape[1]
    tm = _fit(seq, TILES["in_tm"], LANES)
    tiles_per_seq = seq // tm
    return pl.pallas_call(
        _proj_transposed_kernel,
        grid=(n // tm,),
        in_specs=[
            pl.BlockSpec((tm, d), lambda i: (i, 0)),
            pl.BlockSpec((d, tn), lambda i: (0, 0), pipeline_mode=pl.Buffered(1)),
        ],
        out_specs=pl.BlockSpec((None, tn, tm), lambda i: (i // tiles_per_seq, 0, i % tiles_per_seq)),
        out_shape=jax.ShapeDtypeStruct((batch, tn, seq), BF16),
        compiler_params=_params(("parallel",), 56),
        name="proj_transposed",
    )(h, w)


def _regroup_kernel(wt_ref, plain_ref, qk_ref, v_ref, gate_ref, *, mw, fw, mh, fh, k_scale, n_rows):
    o_mi = 4 * mw
    o_fq = o_mi + 2 * mh
    o_ff = o_fq + 3 * fw

    def put(dst, c0, r0, r1, scale=None):
        t = wt_ref[r0:r1, :].T
        dst[:, c0:c0 + (r1 - r0)] = (t if scale is None else t * scale).astype(BF16)

    put(plain_ref, 0, 0, mw)
    put(plain_ref, mw, mw, 2 * mw, k_scale)
    put(plain_ref, 2 * mw, 3 * mw, o_mi)
    put(qk_ref, 0, o_fq, o_fq + 2 * fw)
    put(v_ref, 0, o_fq + 2 * fw, o_ff)
    put(v_ref, fw, 2 * mw, 3 * mw)
    cols = wt_ref.shape[1]
    parts = [wt_ref[o_mi:o_fq, :], jnp.zeros((n_rows - 2 * mh - fh, cols), F32), wt_ref[o_ff:o_ff + fh, :],
             jnp.zeros((LANES - n_rows, cols), F32)]
    gate_ref[...] = jnp.concatenate([p for p in parts if p.shape[0]], axis=0).T.astype(BF16)


def _regroup(wt, *, mw, fw, mh, fh, k_scale, n_rows):
    total, d = wt.shape
    cb = _fit(d, TILES["regroup_cols"], LANES)
    kern = functools.partial(_regroup_kernel, mw=mw, fw=fw, mh=mh, fh=fh, k_scale=k_scale, n_rows=n_rows)
    return pl.pallas_call(
        kern,
        grid=(d // cb,),
        in_specs=[pl.BlockSpec((total, cb), lambda i: (0, i))],
        out_specs=[
            pl.BlockSpec((cb, 3 * mw), lambda i: (i, 0)),
            pl.BlockSpec((cb, 2 * fw), lambda i: (i, 0)),
            pl.BlockSpec((cb, fw + mw), lambda i: (i, 0)),
            pl.BlockSpec((cb, LANES), lambda i: (i, 0)),
        ],
        out_shape=[
            jax.ShapeDtypeStruct((d, 3 * mw), BF16),
            jax.ShapeDtypeStruct((d, 2 * fw), BF16),
            jax.ShapeDtypeStruct((d, fw + mw), BF16),
            jax.ShapeDtypeStruct((d, LANES), BF16),
        ],
        compiler_params=_params(("parallel",), 40),
        name="regroup_w_in",
    )(wt)


def _gates_kernel(zg_ref, b_ref, row_ref, kb_ref, ab_ref, carry_scr, *, n_pass, n_rows, n_attn):
    @pl.when(pl.program_id(1) == 0)
    def _():
        carry_scr[...] = jnp.zeros_like(carry_scr)

    v = zg_ref[...] + b_ref[...]
    blk = v.shape[0]
    lane = lax.broadcasted_iota(jnp.int32, v.shape, 1)
    summed = lane >= n_pass
    ls = jnp.where(summed, _log_sigmoid(v), 0.0)

    def split3(t):
        hi = t.astype(BF16)
        r1 = t - hi.astype(F32)
        mid = r1.astype(BF16)
        return hi, mid, (r1 - mid.astype(F32)).astype(BF16)

    tri = (lax.broadcasted_iota(jnp.int32, (blk, blk), 0) >= lax.broadcasted_iota(jnp.int32, (blk, blk), 1)).astype(BF16)
    sums = _dot(tri, jnp.concatenate(split3(ls), axis=1))
    local = sums[:, :LANES] + sums[:, LANES:2 * LANES] + sums[:, 2 * LANES:]
    cs = local + carry_scr[0:1, :]
    carry_scr[...] = jnp.broadcast_to(cs[blk - 1:blk, :], carry_scr.shape)
    out = jnp.where(summed, cs, v)
    row_ref[...] = out.T[:n_rows, :]

    src = lax.broadcasted_iota(jnp.int32, (N_BIAS_TERMS * LANES, LANES), 0)
    dst = lax.broadcasted_iota(jnp.int32, (N_BIAS_TERMS * LANES, LANES), 1)
    term, col = src // LANES, src % LANES

    def placed(t, first, count):
        place = ((col >= first) & (col < first + count) & (dst == N_BIAS_TERMS * (col - first) + term)).astype(BF16)
        return _dot(jnp.concatenate(split3(t), axis=1), place).astype(BF16)

    kb_ref[...] = placed((local[blk - 1:blk, :] - local) * LOG2E, n_rows - n_attn, n_attn)
    ab_ref[...] = placed(out - pltpu.roll(out, LANES - n_pass, axis=1), 0, n_pass)


def _gates(zg, bias, *, batch, seq, blk, n_pass, n_rows, n_attn):
    nb = seq // blk
    kern = functools.partial(_gates_kernel, n_pass=n_pass, n_rows=n_rows, n_attn=n_attn)
    return pl.pallas_call(
        kern,
        grid=(batch, nb),
        in_specs=[
            pl.BlockSpec((blk, LANES), lambda b, s: (b * nb + s, 0)),
            pl.BlockSpec((1, LANES), lambda b, s: (0, 0)),
        ],
        out_specs=[
            pl.BlockSpec((None, n_rows, blk), lambda b, s: (b, 0, s)),
            pl.BlockSpec((blk, LANES), lambda b, s: (b * nb + s, 0)),
            pl.BlockSpec((blk, LANES), lambda b, s: (b * nb + s, 0)),
        ],
        out_shape=[
            jax.ShapeDtypeStruct((batch, n_rows, seq), F32),
            jax.ShapeDtypeStruct((batch * seq, LANES), BF16),
            jax.ShapeDtypeStruct((batch * seq, LANES), BF16),
        ],
        scratch_shapes=[pltpu.VMEM((8, LANES), F32)],
        compiler_params=_params(("arbitrary", "arbitrary"), 32),
        name="gate_prep",
    )(zg, bias)


def _mlstm_kernel(q_ref, k_ref, o_ref, vt_ref, row_ref, ab_ref, og_ref, out_ref, ct_scr, n_scr, m_scr, *, heads, d):
    @pl.when(pl.program_id(1) == 0)
    def _():
        ct_scr[...] = jnp.zeros_like(ct_scr)
        n_scr[...] = jnp.zeros_like(n_scr)
        m_scr[...] = jnp.zeros_like(m_scr)

    L = q_ref.shape[0]
    below = lax.broadcasted_iota(jnp.int32, (L, L), 0) <= lax.broadcasted_iota(jnp.int32, (L, L), 1)
    term = lax.broadcasted_iota(jnp.int32, (LANES, L), 0)
    for hh in range(heads):
        sl = slice(hh * d, (hh + 1) * d)
        q, k, vt = q_ref[:, sl], k_ref[:, sl], vt_ref[sl, :]
        pick = jnp.where((term >= N_BIAS_TERMS * hh) & (term < N_BIAS_TERMS * (hh + 1)), 1.0, 0.0).astype(BF16)
        a_bc = _dot(ab_ref[...], pick)
        g_row = row_ref[heads + hh:heads + hh + 1, :]
        m_old = m_scr[hh, 0:1, 0:1]

        mm = jnp.maximum(m_old, jnp.max(jnp.where(below, a_bc, -jnp.inf), axis=0, keepdims=True))
        w = jnp.exp(jnp.where(below, a_bc - mm, -jnp.inf))
        st = _dot_nt(k, q) * w
        w_inter = jnp.exp(m_old - mm)
        num = _dot(vt, st.astype(BF16)) + w_inter * _dot_nt(ct_scr[hh].astype(BF16), q)
        nq = _dot_nt(jnp.broadcast_to(n_scr[hh], (8, d)).astype(BF16), q)[0:1, :]
        den = jnp.sum(st, axis=0, keepdims=True) + w_inter * nq
        hout = num * (1.0 / jnp.maximum(jnp.abs(den), jnp.exp(-(g_row + mm))))
        inv_rms = lax.rsqrt(jnp.mean(hout * hout, axis=0, keepdims=True) + EPS)
        gain = jnp.concatenate([og_ref[sl, :]] * (L // LANES), axis=1)
        gate = jax.nn.sigmoid(o_ref[:, sl].astype(F32))
        out_ref[:, sl] = ((hout * inv_rms * gain).T * gate).astype(BF16)

        m_new = mm[:, L - 1:L]
        decay = jnp.exp(m_old - m_new)
        kw = k.astype(F32) * jnp.exp(a_bc[:, :d] - m_new)
        ct_scr[hh] = decay * ct_scr[hh] + _dot(vt, kw.astype(BF16))
        n_scr[hh] = decay * n_scr[hh] + jnp.sum(kw, axis=0, keepdims=True)
        m_scr[hh] = jnp.broadcast_to(m_new, m_scr.shape[1:])


def _mlstm(z, vt, rows, ab, out_g, *, batch, seq, heads, d, vt_block):
    L = _fit(seq, TILES["mlstm_chunk"], LANES)
    assert L == d
    nc = seq // L
    width = heads * d
    n_rows = rows.shape[1]
    kern = functools.partial(_mlstm_kernel, heads=heads, d=d)
    og_b = jnp.broadcast_to(out_g.reshape(width, 1), (width, LANES))

    def zspec(group):
        return pl.BlockSpec((L, width), lambda b, c: (b * nc + c, group))

    return pl.pallas_call(
        kern,
        grid=(batch, nc),
        in_specs=[
            zspec(0), zspec(1), zspec(2),
            pl.BlockSpec((None, width, L), lambda b, c: (b, vt_block, c)),
            pl.BlockSpec((None, n_rows, L), lambda b, c: (b, 0, c)),
            pl.BlockSpec((L, LANES), lambda b, c: (b * nc + c, 0)),
            pl.BlockSpec((width, LANES), lambda b, c: (0, 0)),
        ],
        out_specs=pl.BlockSpec((L, width), lambda b, c: (b * nc + c, 0)),
        out_shape=jax.ShapeDtypeStruct((batch * seq, width), BF16),
        scratch_shapes=[pltpu.VMEM((heads, d, d), F32), pltpu.VMEM((heads, 1, d), F32), pltpu.VMEM((heads, 8, LANES), F32)],
        compiler_params=_params(("parallel", "arbitrary"), 32),
        name="mlstm",
    )(z, z, z, vt, rows, ab, og_b)


def _fox_kernel(fend_ref, fq_ref, win_ref, q_ref, k_ref, kb_ref, vt_ref, f_ref, o_ref, qa_scr, s0_scr, s1_scr, acc_scr, *, tq, tk, heads):
    b, h, i = pl.program_id(0), pl.program_id(1), pl.program_id(2)
    n_diag = tq // tk
    q0 = pl.multiple_of(i * tq, tq)
    lane = lax.broadcasted_iota(jnp.int32, (tq, LANES), 1)
    qa_scr[:, :LANES] = q_ref[...]
    qa_scr[:, LANES:] = jnp.where((lane >= N_BIAS_TERMS * h) & (lane < N_BIAS_TERMS * (h + 1)), 1.0, 0.0).astype(BF16)
    acc_scr[...] = jnp.zeros_like(acc_scr)

    def f_end(k0):
        return f_ref[:, pl.ds(pl.multiple_of(k0 + tk - LANES, LANES), LANES)][:, LANES - 1:LANES]

    f_ref_q = f_end(q0 + tq - tk)

    def key_block(k0):
        return jnp.concatenate([k_ref[pl.ds(k0, tk), :], kb_ref[pl.ds(k0, tk), :]], axis=1)

    def scores(k0, s_ref):
        s_ref[...] = _dot_nt(key_block(k0), qa_scr[...])

    def update(k0, s, carry, c0):
        m, l = carry
        cols = slice(c0, None) if c0 else slice(None)
        off = (f_ref_q - f_end(k0)) * LOG2E
        m_new = jnp.maximum(m[:, cols], jnp.max(s, axis=0, keepdims=True) + off)
        alpha = jnp.exp2(m[:, cols] - m_new)
        p = jnp.exp2(s - (m_new - off))
        l_new = alpha * l[:, cols] + jnp.sum(p, axis=0, keepdims=True)
        acc_scr[:, cols] = alpha * acc_scr[:, cols] + _dot(vt_ref[:, pl.ds(k0, tk)], p.astype(BF16))
        if c0:
            m_new = jnp.concatenate([m[:, :c0], m_new], axis=1)
            l_new = jnp.concatenate([l[:, :c0], l_new], axis=1)
        return m_new, l_new

    row = b * heads + h
    thr = fq_ref[row, i] + win_ref[0, 0]
    n_skip = lax.fori_loop(0, i * n_diag, lambda j, c: c + jnp.where(fend_ref[row, j] >= thr, 1, 0), jnp.int32(0))
    n_proc = i * n_diag - n_skip
    lead = n_proc % 2
    b_first = n_skip + lead

    def block_start(blk):
        return pl.multiple_of(blk * tk, tk)

    scores(block_start(b_first), s0_scr)

    def single(_, carry):
        scores(block_start(n_skip), s1_scr)
        return update(block_start(n_skip), s1_scr[...], carry, 0)

    def pair(blk, carry):
        k0 = block_start(blk)
        scores(k0 + tk, s1_scr)
        carry = update(k0, s0_scr[...], carry, 0)
        scores(k0 + 2 * tk, s0_scr)
        return update(k0 + tk, s1_scr[...], carry, 0)

    carry = (jnp.full((1, tq), NEG_BIG, F32), jnp.zeros((1, tq), F32))
    carry = lax.fori_loop(0, lead, single, carry)
    n_pairs = n_proc // 2
    carry = lax.fori_loop(0, n_pairs // 2, lambda u, c: pair(b_first + 4 * u + 2, pair(b_first + 4 * u, c)), carry)
    carry = lax.fori_loop(n_pairs // 2 * 2, n_pairs, lambda u, c: pair(b_first + 2 * u, c), carry)

    slots = (s0_scr, s1_scr)
    tri = lax.broadcasted_iota(jnp.int32, (tk, tk), 0) <= lax.broadcasted_iota(jnp.int32, (tk, tk), 1)
    for jj in range(n_diag):
        if jj + 1 < n_diag:
            c1 = (jj + 1) * tk
            slots[(jj + 1) % 2][:, c1:] = _dot_nt(key_block(q0 + c1), qa_scr[c1:, :])
        c0 = jj * tk
        s = slots[jj % 2][:, c0:]
        head = jnp.where(tri, s[:, :tk], NEG_BIG)
        s = head if s.shape[1] == tk else jnp.concatenate([head, s[:, tk:]], axis=1)
        carry = update(q0 + c0, s, carry, c0)
    _, l = carry
    o_ref[...] = (acc_scr[...] / l).T.astype(BF16)


def _fox(z, kb, vt, frow, win, *, batch, seq, heads, d, tq, tk, q_group, f_row0):
    assert (tq // tk) % 2 == 0
    nq = seq // tq
    per_group = z.shape[1] // (q_group + 2) // d
    kern = functools.partial(_fox_kernel, tq=tq, tk=tk, heads=heads)
    f4 = frow.reshape(batch, frow.shape[1], 1, seq)
    fh = frow[:, f_row0:f_row0 + heads, :]
    f_end = fh[:, :, tk - 1::tk].reshape(batch * heads, seq // tk)
    f_q = fh[:, :, ::tq].reshape(batch * heads, nq)
    smem = pl.BlockSpec(memory_space=pltpu.SMEM)
    return pl.pallas_call(
        kern,
        grid=(batch, heads, nq),
        in_specs=[
            smem, smem, smem,
            pl.BlockSpec((tq, d), lambda b, h, i: (b * nq + i, q_group * per_group + h)),
            pl.BlockSpec((seq, d), lambda b, h, i: (b, (q_group + 1) * per_group + h)),
            pl.BlockSpec((seq, LANES), lambda b, h, i: (b, 0)),
            pl.BlockSpec((None, d, seq), lambda b, h, i: (b, h, 0)),
            pl.BlockSpec((None, None, 1, seq), lambda b, h, i: (b, f_row0 + h, 0, 0)),
        ],
        out_specs=pl.BlockSpec((tq, d), lambda b, h, i: (b * nq + i, h)),
        out_shape=jax.ShapeDtypeStruct((batch * seq, heads * d), BF16),
        scratch_shapes=[pltpu.VMEM((tq, 2 * LANES), BF16), pltpu.VMEM((tk, tq), F32), pltpu.VMEM((tk, tq), F32), pltpu.VMEM((d, tq), F32)],
        compiler_params=_params(("parallel", "parallel", "arbitrary"), 48),
        name="fox_attention",
    )(f_end, f_q, win, z, z, kb, vt, f4)


def _out_proj_kernel(x_ref, ha_ref, hb_ref, wa_ref, wb_ref, g_ref, x1_ref, h_ref):
    x1 = x_ref[...] + _dot(ha_ref[...], wa_ref[...]) + _dot(hb_ref[...], wb_ref[...])
    x1_ref[...] = x1
    h_ref[...] = _rms(x1, g_ref[...]).astype(BF16)


def _out_proj(x2d, ha, hb, wa, wb, g):
    n, d = x2d.shape
    tm = _fit(n, TILES["out_tm"], BF16_SUBLANES)
    ka, kb = ha.shape[1], hb.shape[1]
    const = dict(pipeline_mode=pl.Buffered(1))
    return pl.pallas_call(
        _out_proj_kernel,
        grid=(n // tm,),
        in_specs=[
            pl.BlockSpec((tm, d), lambda i: (i, 0)),
            pl.BlockSpec((tm, ka), lambda i: (i, 0)),
            pl.BlockSpec((tm, kb), lambda i: (i, 0)),
            pl.BlockSpec((ka, d), lambda i: (0, 0), **const),
            pl.BlockSpec((kb, d), lambda i: (0, 0), **const),
            pl.BlockSpec((1, d), lambda i: (0, 0)),
        ],
        out_specs=[pl.BlockSpec((tm, d), lambda i: (i, 0)), pl.BlockSpec((tm, d), lambda i: (i, 0))],
        out_shape=[jax.ShapeDtypeStruct((n, d), F32), jax.ShapeDtypeStruct((n, d), BF16)],
        compiler_params=_params(("parallel",), 56),
        name="out_proj",
    )(x2d, ha, hb, wa, wb, g)


def _ffn_up_kernel(h_ref, halo_ref, wa_ref, wg_ref, cw_ref, cb_ref, act_ref, wa_scr, wg_scr, a_scr, *, tiles_per_seq, halo):
    i = pl.program_id(1)

    @pl.when(i == 0)
    def _():
        wa_scr[...] = wa_ref[...].astype(BF16)
        wg_scr[...] = wg_ref[...].astype(BF16)

    tm = h_ref.shape[0]
    h = h_ref[...]
    a = _dot(h, wa_scr[...])
    a_prev = _dot(halo_ref[...], wa_scr[...])
    a_prev = jnp.where(i % tiles_per_seq == 0, 0.0, a_prev)
    a_scr[0:halo, :] = a_prev
    a_scr[halo:, :] = a
    y = (cw_ref[2:3, :] * a + cw_ref[1:2, :] * a_scr[pl.ds(halo - 1, tm), :]
         + cw_ref[0:1, :] * a_scr[pl.ds(halo - 2, tm), :] + cb_ref[...])
    gelu = 0.5 * y * (1.0 + lax.erf(y * (0.5 ** 0.5)))
    act_ref[...] = (gelu * _dot(h, wg_scr[...])).astype(BF16)


def _ffn_up(h2, w_up, conv_w, conv_b, *, seq):
    n, d = h2.shape
    d_ff = conv_w.shape[1]
    halo = BF16_SUBLANES
    tm = _fit(seq, TILES["up_tm"], halo)
    tn = _fit(d_ff, TILES["up_tn"], LANES)
    nj = d_ff // tn
    kern = functools.partial(_ffn_up_kernel, tiles_per_seq=seq // tm, halo=halo)
    return pl.pallas_call(
        kern,
        grid=(nj, n // tm),
        in_specs=[
            pl.BlockSpec((tm, d), lambda j, i: (i, 0)),
            pl.BlockSpec((halo, d), lambda j, i: (jnp.maximum(i * (tm // halo) - 1, 0), 0)),
            pl.BlockSpec((d, tn), lambda j, i: (0, j)),
            pl.BlockSpec((d, tn), lambda j, i: (0, nj + j)),
            pl.BlockSpec((conv_w.shape[0], tn), lambda j, i: (0, j)),
            pl.BlockSpec((1, tn), lambda j, i: (0, j)),
        ],
        out_specs=pl.BlockSpec((tm, tn), lambda j, i: (i, j)),
        out_shape=jax.ShapeDtypeStruct((n, d_ff), BF16),
        scratch_shapes=[pltpu.VMEM((d, tn), BF16), pltpu.VMEM((d, tn), BF16), pltpu.VMEM((tm + halo, tn), F32)],
        compiler_params=_params(("arbitrary", "arbitrary"), 56),
        name="ffn_up",
    )(h2, h2, w_up, w_up, conv_w, conv_b)


def _ffn_down_kernel(act_ref, w_ref, x_ref, o_ref):
    o_ref[...] = x_ref[...] + _dot(act_ref[...], w_ref[...])


def _ffn_down(act, w_down, x1):
    n, d_ff = act.shape
    d = w_down.shape[1]
    tm = _fit(n, TILES["down_tm"], BF16_SUBLANES)
    tn = _fit(d, TILES["down_tn"], LANES)
    return pl.pallas_call(
        _ffn_down_kernel,
        grid=(d // tn, n // tm),
        in_specs=[
            pl.BlockSpec((tm, d_ff), lambda j, i: (i, 0)),
            pl.BlockSpec((d_ff, tn), lambda j, i: (0, j)),
            pl.BlockSpec((tm, tn), lambda j, i: (i, j)),
        ],
        out_specs=pl.BlockSpec((tm, tn), lambda j, i: (i, j)),
        out_shape=jax.ShapeDtypeStruct((n, d), F32),
        compiler_params=_params(("arbitrary", "arbitrary"), 56),
        name="ffn_down",
    )(act, w_down, x1)


def _ple_kernel(x_ref, p_ref, g_ref, wg_ref, wp_ref, o_ref):
    x = x_ref[...]
    gate = jax.nn.sigmoid(_dot(_rms(x, g_ref[...]).astype(BF16), wg_ref[...]))
    o_ref[...] = x + gate * _dot(p_ref[...].astype(BF16), wp_ref[...])


def _ple(x2, p2d, g, w_gate, w_proj):
    n, d = x2.shape
    dp = p2d.shape[1]
    tm = _fit(n, TILES["ple_tm"], 8)
    const = dict(pipeline_mode=pl.Buffered(1))
    return pl.pallas_call(
        _ple_kernel,
        grid=(n // tm,),
        in_specs=[
            pl.BlockSpec((tm, d), lambda i: (i, 0)),
            pl.BlockSpec((tm, dp), lambda i: (i, 0)),
            pl.BlockSpec((1, d), lambda i: (0, 0)),
            pl.BlockSpec((d, d), lambda i: (0, 0), **const),
            pl.BlockSpec((dp, d), lambda i: (0, 0), **const),
        ],
        out_specs=pl.BlockSpec((tm, d), lambda i: (i, 0)),
        out_shape=jax.ShapeDtypeStruct((n, d), F32),
        compiler_params=_params(("parallel",), 56),
        name="ple",
    )(x2, p2d, g, w_gate, w_proj)


def kernel(x, p, mix_norm_g, w_in, mlstm_gate_bias, mlstm_out_g, fox_q_g, fox_k_g, fox_f_bias, w_out, ffn_norm_g, w_up, conv_w, conv_b, w_down, ple_norm_g, w_ple_gate, w_ple_proj):
    batch, seq, d_model = x.shape
    depth = w_in.shape[0]
    mh = mlstm_gate_bias.shape[1] // 2
    mw = mlstm_out_g.shape[1]
    md = mw // mh
    fh = fox_f_bias.shape[1]
    fd = fox_q_g.shape[1]
    fw = fh * fd
    n_gate = 2 * mh + fh
    n_rows = -(-n_gate // 8) * 8
    assert mw == fw and fd == LANES and n_rows <= LANES and N_BIAS_TERMS * fh <= LANES
    assert w_in.shape[2] == 4 * mw + 2 * mh + 3 * fw + fh
    tq = _fit(seq, TILES["fox_tq"], LANES)
    tk = _fit(tq, TILES["fox_tk"], LANES)

    xs = x.reshape(batch * seq, d_model)
    for li in range(depth):
        w_plain, w_qk, w_v, w_gate = _regroup(w_in[li].T, mw=mw, fw=fw, mh=mh, fh=fh, k_scale=md ** -0.5, n_rows=n_rows)
        pad = jnp.zeros((d_model, n_rows - n_gate), F32)
        gate_bias = jnp.concatenate([mlstm_gate_bias[li], pad[0], fox_f_bias[li], jnp.zeros((LANES - n_rows,), F32)])[None, :]
        q_gain = (fox_q_g[li] * (fd ** -0.5 * LOG2E))[None, :]
        k_gain = fox_k_g[li][None, :]

        zm, zg, h1 = _proj_plain(xs, mix_norm_g[li][None, :], w_plain, w_gate, tn=mw)
        zf = _proj_headnorm(h1, w_qk, jnp.stack([q_gain, k_gain]), tn=fw)
        vt = _proj_transposed(h1, w_v, batch=batch)
        rows, kb, ab = _gates(zg, gate_bias, batch=batch, seq=seq, blk=tk, n_pass=mh, n_rows=n_rows, n_attn=fh)
        ha = _mlstm(zm, vt, rows, ab, mlstm_out_g[li], batch=batch, seq=seq, heads=mh, d=md, vt_block=1)
        qk_bound = 1.02 * fd * jnp.max(jnp.abs(q_gain)) * jnp.max(jnp.abs(k_gain))
        win = ((2.0 * qk_bound + F32_MIN_EXP) / LOG2E).reshape(1, 1)
        hb = _fox(zf, kb, vt, rows, win, batch=batch, seq=seq, heads=fh, d=fd, tq=tq, tk=tk, q_group=0, f_row0=n_rows - fh)

        wo = w_out[li].astype(BF16)
        x1, h2 = _out_proj(xs, ha, hb, wo[:mw], wo[mw:], ffn_norm_g[li][None, :])
        act = _ffn_up(h2, w_up[li], conv_w[li], conv_b[li][None, :], seq=seq)
        x2 = _ffn_down(act, w_down[li].astype(BF16), x1)
        xs = _ple(x2, p[li].reshape(batch * seq, -1), ple_norm_g[li][None, :], w_ple_gate[li].astype(BF16), w_ple_proj[li].astype(BF16))
    return xs.reshape(batch, seq, d_model)
```

```python
import functools
import math

import jax
import jax.numpy as jnp
from jax import lax
from jax.experimental import pallas as pl
from jax.experimental.pallas import tpu as pltpu

F32 = jnp.float32
BF16 = jnp.bfloat16
EPS = 1e-6
LANES = 128
BF16_SUBLANES = 16
NEG_BIG = -1e30
LOG2E = math.log2(math.e)
N_BIAS_TERMS = 3
F32_MIN_EXP = 127.0
MIB = 1024 * 1024

TILES = dict(
    regroup_cols=256, in_tm=1024, mlstm_chunk=256, fox_tq=1024, fox_tk=512,
    out_tm=512, up_tm=1024, up_tn=512, down_tm=512, down_tn=1024, ple_tm=512,
)


def _fit(n, pref, align):
    t = min(n, pref)
    while t > align and n % t:
        t //= 2
    assert n % t == 0 and (t % align == 0 or t == n), (n, pref, align)
    return t


def _params(semantics, vmem_mib):
    return pltpu.CompilerParams(dimension_semantics=semantics, vmem_limit_bytes=vmem_mib * MIB)


def _rms(x, g):
    return x * lax.rsqrt(jnp.mean(x * x, axis=-1, keepdims=True) + EPS) * g


def _dot(a, b):
    return jnp.dot(a, b, preferred_element_type=F32)


def _dot_nt(a, b):
    return lax.dot_general(a, b, (((1,), (1,)), ((), ())), preferred_element_type=F32)


def _dot_tn(a, b):
    return lax.dot_general(a, b, (((0,), (0,)), ((), ())), preferred_element_type=F32)


def _serpentine(i, j, *, nj):
    return jnp.where(i % 2 == 0, j, nj - 1 - j)


def _log_sigmoid(v):
    return jnp.minimum(v, 0.0) - jnp.log1p(jnp.exp(-jnp.abs(v)))


def _proj_plain_kernel(x_ref, g_ref, wg_ref, w_ref, z_ref, zg_ref, h_ref, h_scr):
    @pl.when(pl.program_id(1) == 0)
    def _():
        hb = _rms(x_ref[...], g_ref[...]).astype(BF16)
        h_scr[...] = hb
        h_ref[...] = hb
        zg_ref[...] = _dot(hb, wg_ref[...])

    z_ref[...] = _dot(h_scr[...], w_ref[...]).astype(BF16)


def _proj_plain(x2d, g, w, w_gate, *, tn):
    n, d = x2d.shape
    tm = _fit(n, TILES["in_tm"], LANES)
    nj = w.shape[1] // tn
    col = functools.partial(_serpentine, nj=nj)
    return pl.pallas_call(
        _proj_plain_kernel,
        grid=(n // tm, nj),
        in_specs=[
            pl.BlockSpec((tm, d), lambda i, j: (i, 0)),
            pl.BlockSpec((1, d), lambda i, j: (0, 0)),
            pl.BlockSpec((d, LANES), lambda i, j: (0, 0)),
            pl.BlockSpec((d, tn), lambda i, j: (0, col(i, j))),
        ],
        out_specs=[
            pl.BlockSpec((tm, tn), lambda i, j: (i, col(i, j))),
            pl.BlockSpec((tm, LANES), lambda i, j: (i, 0)),
            pl.BlockSpec((tm, d), lambda i, j: (i, 0)),
        ],
        out_shape=[
            jax.ShapeDtypeStruct((n, w.shape[1]), BF16),
            jax.ShapeDtypeStruct((n, LANES), F32),
            jax.ShapeDtypeStruct((n, d), BF16),
        ],
        scratch_shapes=[pltpu.VMEM((tm, d), BF16)],
        compiler_params=_params(("parallel", "arbitrary"), 56),
        name="proj_plain",
    )(x2d, g, w_gate, w)


def _proj_headnorm_kernel(h_ref, w_ref, gain_ref, z_ref):
    acc = _dot(h_ref[...], w_ref[...])
    hd = gain_ref.shape[-1]
    for hh in range(acc.shape[1] // hd):
        sl = slice(hh * hd, (hh + 1) * hd)
        z_ref[:, sl] = _rms(acc[:, sl], gain_ref[...]).astype(BF16)


def _proj_headnorm(h, w, gains, *, tn):
    n, d = h.shape
    hd = gains.shape[-1]
    tm = _fit(n, TILES["in_tm"], LANES)
    nj = w.shape[1] // tn
    col = functools.partial(_serpentine, nj=nj)
    return pl.pallas_call(
        _proj_headnorm_kernel,
        grid=(n // tm, nj),
        in_specs=[
            pl.BlockSpec((tm, d), lambda i, j: (i, 0)),
            pl.BlockSpec((d, tn), lambda i, j: (0, col(i, j))),
            pl.BlockSpec((None, 1, hd), lambda i, j: (col(i, j), 0, 0)),
        ],
        out_specs=pl.BlockSpec((tm, tn), lambda i, j: (i, col(i, j))),
        out_shape=jax.ShapeDtypeStruct((n, w.shape[1]), BF16),
        compiler_params=_params(("parallel", "arbitrary"), 56),
        name="proj_headnorm",
    )(h, w, gains)


def _proj_transposed_kernel(h_ref, w_ref, vt_ref):
    vt_ref[...] = lax.dot_general(w_ref[...], h_ref[...], (((0,), (1,)), ((), ())), preferred_element_type=F32).astype(BF16)


def _proj_transposed(h, w, *, batch):
    n, d = h.shape
    seq = n // batch
    tn = w.shape[1]
    tm = _fit(seq, TILES["in_tm"], LANES)
    tiles_per_seq = seq // tm
    return pl.pallas_call(
        _proj_transposed_kernel,
        grid=(n // tm,),
        in_specs=[
            pl.BlockSpec((tm, d), lambda i: (i, 0)),
            pl.BlockSpec((d, tn), lambda i: (0, 0), pipeline_mode=pl.Buffered(1)),
        ],
        out_specs=pl.BlockSpec((None, tn, tm), lambda i: (i // tiles_per_seq, 0, i % tiles_per_seq)),
        out_shape=jax.ShapeDtypeStruct((batch, tn, seq), BF16),
        compiler_params=_params(("parallel",), 56),
        name="proj_transposed",
    )(h, w)


def _regroup_kernel(wt_ref, plain_ref, qk_ref, v_ref, gate_ref, *, mw, fw, mh, fh, k_scale, n_rows):
    o_mi = 4 * mw
    o_fq = o_mi + 2 * mh
    o_ff = o_fq + 3 * fw

    def put(dst, c0, r0, r1, scale=None):
        t = wt_ref[r0:r1, :].T
        dst[:, c0:c0 + (r1 - r0)] = (t if scale is None else t * scale).astype(BF16)

    put(plain_ref, 0, 0, mw)
    put(plain_ref, mw, mw, 2 * mw, k_scale)
    put(plain_ref, 2 * mw, 3 * mw, o_mi)
    put(qk_ref, 0, o_fq, o_fq + 2 * fw)
    put(v_ref, 0, o_fq + 2 * fw, o_ff)
    put(v_ref, fw, 2 * mw, 3 * mw)
    cols = wt_ref.shape[1]
    parts = [wt_ref[o_mi:o_fq, :], jnp.zeros((n_rows - 2 * mh - fh, cols), F32), wt_ref[o_ff:o_ff + fh, :],
             jnp.zeros((LANES - n_rows, cols), F32)]
    gate_ref[...] = jnp.concatenate([p for p in parts if p.shape[0]], axis=0).T.astype(BF16)


def _regroup(wt, *, mw, fw, mh, fh, k_scale, n_rows):
    total, d = wt.shape
    cb = _fit(d, TILES["regroup_cols"], LANES)
    kern = functools.partial(_regroup_kernel, mw=mw, fw=fw, mh=mh, fh=fh, k_scale=k_scale, n_rows=n_rows)
    return pl.pallas_call(
        kern,
        grid=(d // cb,),
        in_specs=[pl.BlockSpec((total, cb), lambda i: (0, i))],
        out_specs=[
            pl.BlockSpec((cb, 3 * mw), lambda i: (i, 0)),
            pl.BlockSpec((cb, 2 * fw), lambda i: (i, 0)),
            pl.BlockSpec((cb, fw + mw), lambda i: (i, 0)),
            pl.BlockSpec((cb, LANES), lambda i: (i, 0)),
        ],
        out_shape=[
            jax.ShapeDtypeStruct((d, 3 * mw), BF16),
            jax.ShapeDtypeStruct((d, 2 * fw), BF16),
            jax.ShapeDtypeStruct((d, fw + mw), BF16),
            jax.ShapeDtypeStruct((d, LANES), BF16),
        ],
        compiler_params=_params(("parallel",), 40),
        name="regroup_w_in",
    )(wt)


def _gates_kernel(zg_ref, b_ref, row_ref, kb_ref, ab_ref, carry_scr, *, n_pass, n_rows, n_attn):
    @pl.when(pl.program_id(1) == 0)
    def _():
        carry_scr[...] = jnp.zeros_like(carry_scr)

    v = zg_ref[...] + b_ref[...]
    blk = v.shape[0]
    lane = lax.broadcasted_iota(jnp.int32, v.shape, 1)
    summed = lane >= n_pass
    ls = jnp.where(summed, _log_sigmoid(v), 0.0)

    def split3(t):
        hi = t.astype(BF16)
        r1 = t - hi.astype(F32)
        mid = r1.astype(BF16)
        return hi, mid, (r1 - mid.astype(F32)).astype(BF16)

    tri = (lax.broadcasted_iota(jnp.int32, (blk, blk), 0) >= lax.broadcasted_iota(jnp.int32, (blk, blk), 1)).astype(BF16)
    sums = _dot(tri, jnp.concatenate(split3(ls), axis=1))
    local = sums[:, :LANES] + sums[:, LANES:2 * LANES] + sums[:, 2 * LANES:]
    cs = local + carry_scr[0:1, :]
    carry_scr[...] = jnp.broadcast_to(cs[blk - 1:blk, :], carry_scr.shape)
    out = jnp.where(summed, cs, v)
    row_ref[...] = out.T[:n_rows, :]

    src = lax.broadcasted_iota(jnp.int32, (N_BIAS_TERMS * LANES, LANES), 0)
    dst = lax.broadcasted_iota(jnp.int32, (N_BIAS_TERMS * LANES, LANES), 1)
    term, col = src // LANES, src % LANES

    def placed(t, first, count):
        place = ((col >= first) & (col < first + count) & (dst == N_BIAS_TERMS * (col - first) + term)).astype(BF16)
        return _dot(jnp.concatenate(split3(t), axis=1), place).astype(BF16)

    kb_ref[...] = placed((local[blk - 1:blk, :] - local) * LOG2E, n_rows - n_attn, n_attn)
    ab_ref[...] = placed(out - pltpu.roll(out, LANES - n_pass, axis=1), 0, n_pass)


def _gates(zg, bias, *, batch, seq, blk, n_pass, n_rows, n_attn):
    nb = seq // blk
    kern = functools.partial(_gates_kernel, n_pass=n_pass, n_rows=n_rows, n_attn=n_attn)
    return pl.pallas_call(
        kern,
        grid=(batch, nb),
        in_specs=[
            pl.BlockSpec((blk, LANES), lambda b, s: (b * nb + s, 0)),
            pl.BlockSpec((1, LANES), lambda b, s: (0, 0)),
        ],
        out_specs=[
            pl.BlockSpec((None, n_rows, blk), lambda b, s: (b, 0, s)),
            pl.BlockSpec((blk, LANES), lambda b, s: (b * nb + s, 0)),
            pl.BlockSpec((blk, LANES), lambda b, s: (b * nb + s, 0)),
        ],
        out_shape=[
            jax.ShapeDtypeStruct((batch, n_rows, seq), F32),
            jax.ShapeDtypeStruct((batch * seq, LANES), BF16),
            jax.ShapeDtypeStruct((batch * seq, LANES), BF16),
        ],
        scratch_shapes=[pltpu.VMEM((8, LANES), F32)],
        compiler_params=_params(("arbitrary", "arbitrary"), 32),
        name="gate_prep",
    )(zg, bias)


def _mlstm_kernel(q_ref, k_ref, o_ref, vt_ref, row_ref, ab_ref, og_ref, out_ref, ct_scr, n_scr, m_scr, *, heads, d):
    @pl.when(pl.program_id(1) == 0)
    def _():
        ct_scr[...] = jnp.zeros_like(ct_scr)
        n_scr[...] = jnp.zeros_like(n_scr)
        m_scr[...] = jnp.zeros_like(m_scr)

    L = q_ref.shape[0]
    below = lax.broadcasted_iota(jnp.int32, (L, L), 0) <= lax.broadcasted_iota(jnp.int32, (L, L), 1)
    term = lax.broadcasted_iota(jnp.int32, (LANES, L), 0)
    for hh in range(heads):
        sl = slice(hh * d, (hh + 1) * d)
        q, k, vt = q_ref[:, sl], k_ref[:, sl], vt_ref[sl, :]
        pick = jnp.where((term >= N_BIAS_TERMS * hh) & (term < N_BIAS_TERMS * (hh + 1)), 1.0, 0.0).astype(BF16)
        a_bc = _dot(ab_ref[...], pick)
        g_row = row_ref[heads + hh:heads + hh + 1, :]
        m_old = m_scr[hh, 0:1, 0:1]

        mm = jnp.maximum(m_old, jnp.max(jnp.where(below, a_bc, -jnp.inf), axis=0, keepdims=True))
        w = jnp.exp(jnp.where(below, a_bc - mm, -jnp.inf))
        st = _dot_nt(k, q) * w
        w_inter = jnp.exp(m_old - mm)
        num = _dot(vt, st.astype(BF16)) + w_inter * _dot_nt(ct_scr[hh].astype(BF16), q)
        nq = _dot_nt(jnp.broadcast_to(n_scr[hh], (8, d)).astype(BF16), q)[0:1, :]
        den = jnp.sum(st, axis=0, keepdims=True) + w_inter * nq
        hout = num * (1.0 / jnp.maximum(jnp.abs(den), jnp.exp(-(g_row + mm))))
        inv_rms = lax.rsqrt(jnp.mean(hout * hout, axis=0, keepdims=True) + EPS)
        gain = jnp.concatenate([og_ref[sl, :]] * (L // LANES), axis=1)
        gate = jax.nn.sigmoid(o_ref[:, sl].astype(F32))
        out_ref[:, sl] = ((hout * inv_rms * gain).T * gate).astype(BF16)

        m_new = mm[:, L - 1:L]
        decay = jnp.exp(m_old - m_new)
        kw = k.astype(F32) * jnp.exp(a_bc[:, :d] - m_new)
        ct_scr[hh] = decay * ct_scr[hh] + _dot(vt, kw.astype(BF16))
        n_scr[hh] = decay * n_scr[hh] + jnp.sum(kw, axis=0, keepdims=True)
        m_scr[hh] = jnp.broadcast_to(m_new, m_scr.shape[1:])


def _mlstm(z, vt, rows, ab, out_g, *, batch, seq, heads, d, vt_block):
    L = _fit(seq, TILES["mlstm_chunk"], LANES)
    assert L == d
    nc = seq // L
    width = heads * d
    n_rows = rows.shape[1]
    kern = functools.partial(_mlstm_kernel, heads=heads, d=d)
    og_b = jnp.broadcast_to(out_g.reshape(width, 1), (width, LANES))

    def zspec(group):
        return pl.BlockSpec((L, width), lambda b, c: (b * nc + c, group))

    return pl.pallas_call(
        kern,
        grid=(batch, nc),
        in_specs=[
            zspec(0), zspec(1), zspec(2),
            pl.BlockSpec((None, width, L), lambda b, c: (b, vt_block, c)),
            pl.BlockSpec((None, n_rows, L), lambda b, c: (b, 0, c)),
            pl.BlockSpec((L, LANES), lambda b, c: (b * nc + c, 0)),
            pl.BlockSpec((width, LANES), lambda b, c: (0, 0)),
        ],
        out_specs=pl.BlockSpec((L, width), lambda b, c: (b * nc + c, 0)),
        out_shape=jax.ShapeDtypeStruct((batch * seq, width), BF16),
        scratch_shapes=[pltpu.VMEM((heads, d, d), F32), pltpu.VMEM((heads, 1, d), F32), pltpu.VMEM((heads, 8, LANES), F32)],
        compiler_params=_params(("parallel", "arbitrary"), 32),
        name="mlstm",
    )(z, z, z, vt, rows, ab, og_b)


def _fox_kernel(fend_ref, fq_ref, win_ref, q_ref, k_ref, kb_ref, vt_ref, f_ref, o_ref, qa_scr, s0_scr, s1_scr, acc_scr, *, tq, tk, heads):
    b, h, i = pl.program_id(0), pl.program_id(1), pl.program_id(2)
    n_diag = tq // tk
    q0 = pl.multiple_of(i * tq, tq)
    lane = lax.broadcasted_iota(jnp.int32, (tq, LANES), 1)
    qa_scr[:, :LANES] = q_ref[...]
    qa_scr[:, LANES:] = jnp.where((lane >= N_BIAS_TERMS * h) & (lane < N_BIAS_TERMS * (h + 1)), 1.0, 0.0).astype(BF16)
    acc_scr[...] = jnp.zeros_like(acc_scr)

    def f_end(k0):
        return f_ref[:, pl.ds(pl.multiple_of(k0 + tk - LANES, LANES), LANES)][:, LANES - 1:LANES]

    f_ref_q = f_end(q0 + tq - tk)

    def key_block(k0):
        return jnp.concatenate([k_ref[pl.ds(k0, tk), :], kb_ref[pl.ds(k0, tk), :]], axis=1)

    def scores(k0, s_ref):
        s_ref[...] = _dot_nt(key_block(k0), qa_scr[...])

    def update(k0, s, carry, c0):
        m, l = carry
        cols = slice(c0, None) if c0 else slice(None)
        off = (f_ref_q - f_end(k0)) * LOG2E
        m_new = jnp.maximum(m[:, cols], jnp.max(s, axis=0, keepdims=True) + off)
        alpha = jnp.exp2(m[:, cols] - m_new)
        p = jnp.exp2(s - (m_new - off))
        l_new = alpha * l[:, cols] + jnp.sum(p, axis=0, keepdims=True)
        acc_scr[:, cols] = alpha * acc_scr[:, cols] + _dot(vt_ref[:, pl.ds(k0, tk)], p.astype(BF16))
        if c0:
            m_new = jnp.concatenate([m[:, :c0], m_new], axis=1)
            l_new = jnp.concatenate([l[:, :c0], l_new], axis=1)
        return m_new, l_new

    row = b * heads + h
    thr = fq_ref[row, i] + win_ref[0, 0]
    n_skip = lax.fori_loop(0, i * n_diag, lambda j, c: c + jnp.where(fend_ref[row, j] >= thr, 1, 0), jnp.int32(0))
    n_proc = i * n_diag - n_skip
    lead = n_proc % 2
    b_first = n_skip + lead

    def block_start(blk):
        return pl.multiple_of(blk * tk, tk)

    scores(block_start(b_first), s0_scr)

    def single(_, carry):
        scores(block_start(n_skip), s1_scr)
        return update(block_start(n_skip), s1_scr[...], carry, 0)

    def pair(blk, carry):
        k0 = block_start(blk)
        scores(k0 + tk, s1_scr)
        carry = update(k0, s0_scr[...], carry, 0)
        scores(k0 + 2 * tk, s0_scr)
        return update(k0 + tk, s1_scr[...], carry, 0)

    carry = (jnp.full((1, tq), NEG_BIG, F32), jnp.zeros((1, tq), F32))
    carry = lax.fori_loop(0, lead, single, carry)
    n_pairs = n_proc // 2
    carry = lax.fori_loop(0, n_pairs // 2, lambda u, c: pair(b_first + 4 * u + 2, pair(b_first + 4 * u, c)), carry)
    carry = lax.fori_loop(n_pairs // 2 * 2, n_pairs, lambda u, c: pair(b_first + 2 * u, c), carry)

    slots = (s0_scr, s1_scr)
    tri = lax.broadcasted_iota(jnp.int32, (tk, tk), 0) <= lax.broadcasted_iota(jnp.int32, (tk, tk), 1)
    for jj in range(n_diag):
        if jj + 1 < n_diag:
            c1 = (jj + 1) * tk
            slots[(jj + 1) % 2][:, c1:] = _dot_nt(key_block(q0 + c1), qa_scr[c1:, :])
        c0 = jj * tk
        s = slots[jj % 2][:, c0:]
        head = jnp.where(tri, s[:, :tk], NEG_BIG)
        s = head if s.shape[1] == tk else jnp.concatenate([head, s[:, tk:]], axis=1)
        carry = update(q0 + c0, s, carry, c0)
    _, l = carry
    o_ref[...] = (acc_scr[...] / l).T.astype(BF16)


def _fox(z, kb, vt, frow, win, *, batch, seq, heads, d, tq, tk, q_group, f_row0):
    assert (tq // tk) % 2 == 0
    nq = seq // tq
    per_group = z.shape[1] // (q_group + 2) // d
    kern = functools.partial(_fox_kernel, tq=tq, tk=tk, heads=heads)
    f4 = frow.reshape(batch, frow.shape[1], 1, seq)
    fh = frow[:, f_row0:f_row0 + heads, :]
    f_end = fh[:, :, tk - 1::tk].reshape(batch * heads, seq // tk)
    f_q = fh[:, :, ::tq].reshape(batch * heads, nq)
    smem = pl.BlockSpec(memory_space=pltpu.SMEM)
    return pl.pallas_call(
        kern,
        grid=(batch, heads, nq),
        in_specs=[
            smem, smem, smem,
            pl.BlockSpec((tq, d), lambda b, h, i: (b * nq + i, q_group * per_group + h)),
            pl.BlockSpec((seq, d), lambda b, h, i: (b, (q_group + 1) * per_group + h)),
            pl.BlockSpec((seq, LANES), lambda b, h, i: (b, 0)),
            pl.BlockSpec((None, d, seq), lambda b, h, i: (b, h, 0)),
            pl.BlockSpec((None, None, 1, seq), lambda b, h, i: (b, f_row0 + h, 0, 0)),
        ],
        out_specs=pl.BlockSpec((tq, d), lambda b, h, i: (b * nq + i, h)),
        out_shape=jax.ShapeDtypeStruct((batch * seq, heads * d), BF16),
        scratch_shapes=[pltpu.VMEM((tq, 2 * LANES), BF16), pltpu.VMEM((tk, tq), F32), pltpu.VMEM((tk, tq), F32), pltpu.VMEM((d, tq), F32)],
        compiler_params=_params(("parallel", "parallel", "arbitrary"), 48),
        name="fox_attention",
    )(f_end, f_q, win, z, z, kb, vt, f4)


def _out_proj_kernel(x_ref, ha_ref, hb_ref, wa_ref, wb_ref, g_ref, x1_ref, h_ref):
    x1 = x_ref[...] + _dot(ha_ref[...], wa_ref[...]) + _dot(hb_ref[...], wb_ref[...])
    x1_ref[...] = x1
    h_ref[...] = _rms(x1, g_ref[...]).astype(BF16)


def _out_proj(x2d, ha, hb, wa, wb, g):
    n, d = x2d.shape
    tm = _fit(n, TILES["out_tm"], BF16_SUBLANES)
    ka, kb = ha.shape[1], hb.shape[1]
    const = dict(pipeline_mode=pl.Buffered(1))
    return pl.pallas_call(
        _out_proj_kernel,
        grid=(n // tm,),
        in_specs=[
            pl.BlockSpec((tm, d), lambda i: (i, 0)),
            pl.BlockSpec((tm, ka), lambda i: (i, 0)),
            pl.BlockSpec((tm, kb), lambda i: (i, 0)),
            pl.BlockSpec((ka, d), lambda i: (0, 0), **const),
            pl.BlockSpec((kb, d), lambda i: (0, 0), **const),
            pl.BlockSpec((1, d), lambda i: (0, 0)),
        ],
        out_specs=[pl.BlockSpec((tm, d), lambda i: (i, 0)), pl.BlockSpec((tm, d), lambda i: (i, 0))],
        out_shape=[jax.ShapeDtypeStruct((n, d), F32), jax.ShapeDtypeStruct((n, d), BF16)],
        compiler_params=_params(("parallel",), 56),
        name="out_proj",
    )(x2d, ha, hb, wa, wb, g)


def _ffn_up_kernel(h_ref, halo_ref, wa_ref, wg_ref, cw_ref, cb_ref, act_ref, wa_scr, wg_scr, a_scr, *, tiles_per_seq, halo):
    i = _serpentine(pl.program_id(0), pl.program_id(1), nj=pl.num_programs(1))

    @pl.when(pl.program_id(1) == 0)
    def _():
        wa_scr[...] = wa_ref[...].astype(BF16)
        wg_scr[...] = wg_ref[...].astype(BF16)

    tm = h_ref.shape[0]
    h = h_ref[...]
    a = _dot(h, wa_scr[...])
    a_prev = _dot(halo_ref[...], wa_scr[...])
    a_prev = jnp.where(i % tiles_per_seq == 0, 0.0, a_prev)
    a_scr[0:halo, :] = a_prev
    a_scr[halo:, :] = a
    y = (cw_ref[2:3, :] * a + cw_ref[1:2, :] * a_scr[pl.ds(halo - 1, tm), :]
         + cw_ref[0:1, :] * a_scr[pl.ds(halo - 2, tm), :] + cb_ref[...])
    gelu = 0.5 * y * (1.0 + lax.erf(y * (0.5 ** 0.5)))
    act_ref[...] = (gelu * _dot(h, wg_scr[...])).astype(BF16)


def _ffn_up(h2, w_up, conv_w, conv_b, *, seq):
    n, d = h2.shape
    d_ff = conv_w.shape[1]
    halo = BF16_SUBLANES
    tm = _fit(seq, TILES["up_tm"], halo)
    tn = _fit(d_ff, TILES["up_tn"], LANES)
    nj = d_ff // tn
    kern = functools.partial(_ffn_up_kernel, tiles_per_seq=seq // tm, halo=halo)
    row = functools.partial(_serpentine, nj=n // tm)
    return pl.pallas_call(
        kern,
        grid=(nj, n // tm),
        in_specs=[
            pl.BlockSpec((tm, d), lambda j, i: (row(j, i), 0)),
            pl.BlockSpec((halo, d), lambda j, i: (jnp.maximum(row(j, i) * (tm // halo) - 1, 0), 0)),
            pl.BlockSpec((d, tn), lambda j, i: (0, j)),
            pl.BlockSpec((d, tn), lambda j, i: (0, nj + j)),
            pl.BlockSpec((conv_w.shape[0], tn), lambda j, i: (0, j)),
            pl.BlockSpec((1, tn), lambda j, i: (0, j)),
        ],
        out_specs=pl.BlockSpec((tm, tn), lambda j, i: (row(j, i), j)),
        out_shape=jax.ShapeDtypeStruct((n, d_ff), BF16),
        scratch_shapes=[pltpu.VMEM((d, tn), BF16), pltpu.VMEM((d, tn), BF16), pltpu.VMEM((tm + halo, tn), F32)],
        compiler_params=_params(("arbitrary", "arbitrary"), 56),
        name="ffn_up",
    )(h2, h2, w_up, w_up, conv_w, conv_b)


def _ffn_down_kernel(act_ref, w_ref, x_ref, o_ref):
    o_ref[...] = x_ref[...] + _dot(act_ref[...], w_ref[...])


def _ffn_down(act, w_down, x1):
    n, d_ff = act.shape
    d = w_down.shape[1]
    tm = _fit(n, TILES["down_tm"], BF16_SUBLANES)
    tn = _fit(d, TILES["down_tn"], LANES)
    return pl.pallas_call(
        _ffn_down_kernel,
        grid=(d // tn, n // tm),
        in_specs=[
            pl.BlockSpec((tm, d_ff), lambda j, i: (i, 0)),
            pl.BlockSpec((d_ff, tn), lambda j, i: (0, j)),
            pl.BlockSpec((tm, tn), lambda j, i: (i, j)),
        ],
        out_specs=pl.BlockSpec((tm, tn), lambda j, i: (i, j)),
        out_shape=jax.ShapeDtypeStruct((n, d), F32),
        compiler_params=_params(("arbitrary", "arbitrary"), 56),
        name="ffn_down",
    )(act, w_down, x1)


def _ple_kernel(x_ref, p_ref, g_ref, wg_ref, wp_ref, o_ref):
    x = x_ref[...]
    gate = jax.nn.sigmoid(_dot(_rms(x, g_ref[...]).astype(BF16), wg_ref[...]))
    o_ref[...] = x + gate * _dot(p_ref[...].astype(BF16), wp_ref[...])


def _ple(x2, p2d, g, w_gate, w_proj):
    n, d = x2.shape
    dp = p2d.shape[1]
    tm = _fit(n, TILES["ple_tm"], 8)
    const = dict(pipeline_mode=pl.Buffered(1))
    return pl.pallas_call(
        _ple_kernel,
        grid=(n // tm,),
        in_specs=[
            pl.BlockSpec((tm, d), lambda i: (i, 0)),
            pl.BlockSpec((tm, dp), lambda i: (i, 0)),
            pl.BlockSpec((1, d), lambda i: (0, 0)),
            pl.BlockSpec((d, d), lambda i: (0, 0), **const),
            pl.BlockSpec((dp, d), lambda i: (0, 0), **const),
        ],
        out_specs=pl.BlockSpec((tm, d), lambda i: (i, 0)),
        out_shape=jax.ShapeDtypeStruct((n, d), F32),
        compiler_params=_params(("parallel",), 56),
        name="ple",
    )(x2, p2d, g, w_gate, w_proj)


def kernel(x, p, mix_norm_g, w_in, mlstm_gate_bias, mlstm_out_g, fox_q_g, fox_k_g, fox_f_bias, w_out, ffn_norm_g, w_up, conv_w, conv_b, w_down, ple_norm_g, w_ple_gate, w_ple_proj):
    batch, seq, d_model = x.shape
    depth = w_in.shape[0]
    mh = mlstm_gate_bias.shape[1] // 2
    mw = mlstm_out_g.shape[1]
    md = mw // mh
    fh = fox_f_bias.shape[1]
    fd = fox_q_g.shape[1]
    fw = fh * fd
    n_gate = 2 * mh + fh
    n_rows = -(-n_gate // 8) * 8
    assert mw == fw and fd == LANES and n_rows <= LANES and N_BIAS_TERMS * fh <= LANES
    assert w_in.shape[2] == 4 * mw + 2 * mh + 3 * fw + fh
    tq = _fit(seq, TILES["fox_tq"], LANES)
    tk = _fit(tq, TILES["fox_tk"], LANES)

    xs = x.reshape(batch * seq, d_model)
    for li in range(depth):
        w_plain, w_qk, w_v, w_gate = _regroup(w_in[li].T, mw=mw, fw=fw, mh=mh, fh=fh, k_scale=md ** -0.5, n_rows=n_rows)
        pad = jnp.zeros((d_model, n_rows - n_gate), F32)
        gate_bias = jnp.concatenate([mlstm_gate_bias[li], pad[0], fox_f_bias[li], jnp.zeros((LANES - n_rows,), F32)])[None, :]
        q_gain = (fox_q_g[li] * (fd ** -0.5 * LOG2E))[None, :]
        k_gain = fox_k_g[li][None, :]

        zm, zg, h1 = _proj_plain(xs, mix_norm_g[li][None, :], w_plain, w_gate, tn=mw)
        zf = _proj_headnorm(h1, w_qk, jnp.stack([q_gain, k_gain]), tn=fw)
        vt = _proj_transposed(h1, w_v, batch=batch)
        rows, kb, ab = _gates(zg, gate_bias, batch=batch, seq=seq, blk=tk, n_pass=mh, n_rows=n_rows, n_attn=fh)
        ha = _mlstm(zm, vt, rows, ab, mlstm_out_g[li], batch=batch, seq=seq, heads=mh, d=md, vt_block=1)
        qk_bound = 1.02 * fd * jnp.max(jnp.abs(q_gain)) * jnp.max(jnp.abs(k_gain))
        win = ((2.0 * qk_bound + F32_MIN_EXP) / LOG2E).reshape(1, 1)
        hb = _fox(zf, kb, vt, rows, win, batch=batch, seq=seq, heads=fh, d=fd, tq=tq, tk=tk, q_group=0, f_row0=n_rows - fh)

        wo = w_out[li].astype(BF16)
        x1, h2 = _out_proj(xs, ha, hb, wo[:mw], wo[mw:], ffn_norm_g[li][None, :])
        act = _ffn_up(h2, w_up[li], conv_w[li], conv_b[li][None, :], seq=seq)
        x2 = _ffn_down(act, w_down[li].astype(BF16), x1)
        xs = _ple(x2, p[li].reshape(batch * seq, -1), ple_norm_g[li][None, :], w_ple_gate[li].astype(BF16), w_ple_proj[li].astype(BF16))
    return xs.reshape(batch, seq, d_model)
```

```python
import functools
import math

import jax
import jax.numpy as jnp
from jax import lax
from jax.experimental import pallas as pl
from jax.experimental.pallas import tpu as pltpu

F32 = jnp.float32
BF16 = jnp.bfloat16
EPS = 1e-6
LANES = 128
BF16_SUBLANES = 16
NEG_BIG = -1e30
LOG2E = math.log2(math.e)
N_BIAS_TERMS = 3
F32_MIN_EXP = 127.0
MIB = 1024 * 1024

TILES = dict(
    regroup_cols=256, in_tm=1024, mlstm_chunk=256, fox_tq=1024, fox_tk=512,
    out_tm=512, up_tm=1024, up_tn=512, down_tm=512, down_tn=1024, ple_tm=512,
)


def _fit(n, pref, align):
    t = min(n, pref)
    while t > align and n % t:
        t //= 2
    assert n % t == 0 and (t % align == 0 or t == n), (n, pref, align)
    return t


def _params(semantics, vmem_mib):
    return pltpu.CompilerParams(dimension_semantics=semantics, vmem_limit_bytes=vmem_mib * MIB)


def _rms(x, g):
    return x * lax.rsqrt(jnp.mean(x * x, axis=-1, keepdims=True) + EPS) * g


def _dot(a, b):
    return jnp.dot(a, b, preferred_element_type=F32)


def _dot_nt(a, b):
    return lax.dot_general(a, b, (((1,), (1,)), ((), ())), preferred_element_type=F32)


def _dot_tn(a, b):
    return lax.dot_general(a, b, (((0,), (0,)), ((), ())), preferred_element_type=F32)


def _serpentine(i, j, *, nj):
    return jnp.where(i % 2 == 0, j, nj - 1 - j)


def _log_sigmoid(v):
    return jnp.minimum(v, 0.0) - jnp.log1p(jnp.exp(-jnp.abs(v)))


def _proj_plain_kernel(x_ref, g_ref, wg_ref, w_ref, z_ref, zg_ref, h_ref, h_scr):
    @pl.when(pl.program_id(1) == 0)
    def _():
        hb = _rms(x_ref[...], g_ref[...]).astype(BF16)
        h_scr[...] = hb
        h_ref[...] = hb
        zg_ref[...] = _dot(hb, wg_ref[...])

    z_ref[...] = _dot(h_scr[...], w_ref[...]).astype(BF16)


def _proj_plain(x2d, g, w, w_gate, *, tn):
    n, d = x2d.shape
    tm = _fit(n, TILES["in_tm"], LANES)
    nj = w.shape[1] // tn
    col = functools.partial(_serpentine, nj=nj)
    return pl.pallas_call(
        _proj_plain_kernel,
        grid=(n // tm, nj),
        in_specs=[
            pl.BlockSpec((tm, d), lambda i, j: (i, 0)),
            pl.BlockSpec((1, d), lambda i, j: (0, 0)),
            pl.BlockSpec((d, LANES), lambda i, j: (0, 0)),
            pl.BlockSpec((d, tn), lambda i, j: (0, col(i, j))),
        ],
        out_specs=[
            pl.BlockSpec((tm, tn), lambda i, j: (i, col(i, j))),
            pl.BlockSpec((tm, LANES), lambda i, j: (i, 0)),
            pl.BlockSpec((tm, d), lambda i, j: (i, 0)),
        ],
        out_shape=[
            jax.ShapeDtypeStruct((n, w.shape[1]), BF16),
            jax.ShapeDtypeStruct((n, LANES), F32),
            jax.ShapeDtypeStruct((n, d), BF16),
        ],
        scratch_shapes=[pltpu.VMEM((tm, d), BF16)],
        compiler_params=_params(("parallel", "arbitrary"), 56),
        name="proj_plain",
    )(x2d, g, w_gate, w)


def _proj_headnorm_kernel(h_ref, w_ref, gain_ref, z_ref):
    acc = _dot(h_ref[...], w_ref[...])
    hd = gain_ref.shape[-1]
    for hh in range(acc.shape[1] // hd):
        sl = slice(hh * hd, (hh + 1) * hd)
        z_ref[:, sl] = _rms(acc[:, sl], gain_ref[...]).astype(BF16)


def _proj_headnorm(h, w, gains, *, tn):
    n, d = h.shape
    hd = gains.shape[-1]
    tm = _fit(n, TILES["in_tm"], LANES)
    nj = w.shape[1] // tn
    col = functools.partial(_serpentine, nj=nj)
    return pl.pallas_call(
        _proj_headnorm_kernel,
        grid=(n // tm, nj),
        in_specs=[
            pl.BlockSpec((tm, d), lambda i, j: (i, 0)),
            pl.BlockSpec((d, tn), lambda i, j: (0, col(i, j))),
            pl.BlockSpec((None, 1, hd), lambda i, j: (col(i, j), 0, 0)),
        ],
        out_specs=pl.BlockSpec((tm, tn), lambda i, j: (i, col(i, j))),
        out_shape=jax.ShapeDtypeStruct((n, w.shape[1]), BF16),
        compiler_params=_params(("parallel", "arbitrary"), 56),
        name="proj_headnorm",
    )(h, w, gains)


def _proj_transposed_kernel(h_ref, w_ref, vt_ref):
    vt_ref[...] = lax.dot_general(w_ref[...], h_ref[...], (((0,), (1,)), ((), ())), preferred_element_type=F32).astype(BF16)


def _proj_transposed(h, w, *, batch):
    n, d = h.shape
    seq = n // batch
    tn = w.shape[1]
    tm = _fit(seq, TILES["in_tm"], LANES)
    tiles_per_seq = seq // tm
    return pl.pallas_call(
        _proj_transposed_kernel,
        grid=(n // tm,),
        in_specs=[
            pl.BlockSpec((tm, d), lambda i: (i, 0)),
            pl.BlockSpec((d, tn), lambda i: (0, 0), pipeline_mode=pl.Buffered(1)),
        ],
        out_specs=pl.BlockSpec((None, tn, tm), lambda i: (i // tiles_per_seq, 0, i % tiles_per_seq)),
        out_shape=jax.ShapeDtypeStruct((batch, tn, seq), BF16),
        compiler_params=_params(("parallel",), 56),
        name="proj_transposed",
    )(h, w)


def _regroup_kernel(wt_ref, plain_ref, qk_ref, v_ref, gate_ref, *, mw, fw, mh, fh, k_scale, n_rows):
    o_mi = 4 * mw
    o_fq = o_mi + 2 * mh
    o_ff = o_fq + 3 * fw

    def put(dst, c0, r0, r1, scale=None):
        t = wt_ref[r0:r1, :].T
        dst[:, c0:c0 + (r1 - r0)] = (t if scale is None else t * scale).astype(BF16)

    put(plain_ref, 0, 0, mw)
    put(plain_ref, mw, mw, 2 * mw, k_scale)
    put(plain_ref, 2 * mw, 3 * mw, o_mi)
    put(qk_ref, 0, o_fq, o_fq + 2 * fw)
    put(v_ref, 0, o_fq + 2 * fw, o_ff)
    put(v_ref, fw, 2 * mw, 3 * mw)
    cols = wt_ref.shape[1]
    parts = [wt_ref[o_mi:o_fq, :], jnp.zeros((n_rows - 2 * mh - fh, cols), F32), wt_ref[o_ff:o_ff + fh, :],
             jnp.zeros((LANES - n_rows, cols), F32)]
    gate_ref[...] = jnp.concatenate([p for p in parts if p.shape[0]], axis=0).T.astype(BF16)


def _regroup(wt, *, mw, fw, mh, fh, k_scale, n_rows):
    total, d = wt.shape
    cb = _fit(d, TILES["regroup_cols"], LANES)
    kern = functools.partial(_regroup_kernel, mw=mw, fw=fw, mh=mh, fh=fh, k_scale=k_scale, n_rows=n_rows)
    return pl.pallas_call(
        kern,
        grid=(d // cb,),
        in_specs=[pl.BlockSpec((total, cb), lambda i: (0, i))],
        out_specs=[
            pl.BlockSpec((cb, 3 * mw), lambda i: (i, 0)),
            pl.BlockSpec((cb, 2 * fw), lambda i: (i, 0)),
            pl.BlockSpec((cb, fw + mw), lambda i: (i, 0)),
            pl.BlockSpec((cb, LANES), lambda i: (i, 0)),
        ],
        out_shape=[
            jax.ShapeDtypeStruct((d, 3 * mw), BF16),
            jax.ShapeDtypeStruct((d, 2 * fw), BF16),
            jax.ShapeDtypeStruct((d, fw + mw), BF16),
            jax.ShapeDtypeStruct((d, LANES), BF16),
        ],
        compiler_params=_params(("parallel",), 40),
        name="regroup_w_in",
    )(wt)


def _gates_kernel(zg_ref, b_ref, row_ref, kb_ref, ab_ref, carry_scr, *, n_pass, n_rows, n_attn):
    @pl.when(pl.program_id(1) == 0)
    def _():
        carry_scr[...] = jnp.zeros_like(carry_scr)

    v = zg_ref[...] + b_ref[...]
    blk = v.shape[0]
    lane = lax.broadcasted_iota(jnp.int32, v.shape, 1)
    summed = lane >= n_pass
    ls = jnp.where(summed, _log_sigmoid(v), 0.0)

    def split3(t):
        hi = t.astype(BF16)
        r1 = t - hi.astype(F32)
        mid = r1.astype(BF16)
        return hi, mid, (r1 - mid.astype(F32)).astype(BF16)

    tri = (lax.broadcasted_iota(jnp.int32, (blk, blk), 0) >= lax.broadcasted_iota(jnp.int32, (blk, blk), 1)).astype(BF16)
    sums = _dot(tri, jnp.concatenate(split3(ls), axis=1))
    local = sums[:, :LANES] + sums[:, LANES:2 * LANES] + sums[:, 2 * LANES:]
    cs = local + carry_scr[0:1, :]
    carry_scr[...] = jnp.broadcast_to(cs[blk - 1:blk, :], carry_scr.shape)
    out = jnp.where(summed, cs, v)
    row_ref[...] = out.T[:n_rows, :]

    src = lax.broadcasted_iota(jnp.int32, (N_BIAS_TERMS * LANES, LANES), 0)
    dst = lax.broadcasted_iota(jnp.int32, (N_BIAS_TERMS * LANES, LANES), 1)
    term, col = src // LANES, src % LANES

    def placed(t, first, count):
        place = ((col >= first) & (col < first + count) & (dst == N_BIAS_TERMS * (col - first) + term)).astype(BF16)
        return _dot(jnp.concatenate(split3(t), axis=1), place).astype(BF16)

    kb_ref[...] = placed((local[blk - 1:blk, :] - local) * LOG2E, n_rows - n_attn, n_attn)
    ab_ref[...] = placed(out - pltpu.roll(out, LANES - n_pass, axis=1), 0, n_pass)


def _gates(zg, bias, *, batch, seq, blk, n_pass, n_rows, n_attn):
    nb = seq // blk
    kern = functools.partial(_gates_kernel, n_pass=n_pass, n_rows=n_rows, n_attn=n_attn)
    return pl.pallas_call(
        kern,
        grid=(batch, nb),
        in_specs=[
            pl.BlockSpec((blk, LANES), lambda b, s: (b * nb + s, 0)),
            pl.BlockSpec((1, LANES), lambda b, s: (0, 0)),
        ],
        out_specs=[
            pl.BlockSpec((None, n_rows, blk), lambda b, s: (b, 0, s)),
            pl.BlockSpec((blk, LANES), lambda b, s: (b * nb + s, 0)),
            pl.BlockSpec((blk, LANES), lambda b, s: (b * nb + s, 0)),
        ],
        out_shape=[
            jax.ShapeDtypeStruct((batch, n_rows, seq), F32),
            jax.ShapeDtypeStruct((batch * seq, LANES), BF16),
            jax.ShapeDtypeStruct((batch * seq, LANES), BF16),
        ],
        scratch_shapes=[pltpu.VMEM((8, LANES), F32)],
        compiler_params=_params(("arbitrary", "arbitrary"), 32),
        name="gate_prep",
    )(zg, bias)


def _mlstm_kernel(q_ref, k_ref, o_ref, vt_ref, row_ref, ab_ref, og_ref, out_ref, ct_scr, n_scr, m_scr, *, heads, d):
    @pl.when(pl.program_id(1) == 0)
    def _():
        ct_scr[...] = jnp.zeros_like(ct_scr)
        n_scr[...] = jnp.zeros_like(n_scr)
        m_scr[...] = jnp.zeros_like(m_scr)

    L = q_ref.shape[0]
    below = lax.broadcasted_iota(jnp.int32, (L, L), 0) <= lax.broadcasted_iota(jnp.int32, (L, L), 1)
    term = lax.broadcasted_iota(jnp.int32, (LANES, L), 0)
    for hh in range(heads):
        sl = slice(hh * d, (hh + 1) * d)
        q, k, vt = q_ref[:, sl], k_ref[:, sl], vt_ref[sl, :]
        pick = jnp.where((term >= N_BIAS_TERMS * hh) & (term < N_BIAS_TERMS * (hh + 1)), 1.0, 0.0).astype(BF16)
        a_bc = _dot(ab_ref[...], pick)
        g_row = row_ref[heads + hh:heads + hh + 1, :]
        m_old = m_scr[hh, 0:1, 0:1]

        mm = jnp.maximum(m_old, jnp.max(jnp.where(below, a_bc, -jnp.inf), axis=0, keepdims=True))
        w = jnp.exp(jnp.where(below, a_bc - mm, -jnp.inf))
        st = _dot_nt(k, q) * w
        w_inter = jnp.exp(m_old - mm)
        num = _dot(vt, st.astype(BF16)) + w_inter * _dot_nt(ct_scr[hh].astype(BF16), q)
        nq = _dot_nt(jnp.broadcast_to(n_scr[hh], (8, d)).astype(BF16), q)[0:1, :]
        den = jnp.sum(st, axis=0, keepdims=True) + w_inter * nq
        hout = num * (1.0 / jnp.maximum(jnp.abs(den), jnp.exp(-(g_row + mm))))
        inv_rms = lax.rsqrt(jnp.mean(hout * hout, axis=0, keepdims=True) + EPS)
        gain = jnp.concatenate([og_ref[sl, :]] * (L // LANES), axis=1)
        gate = jax.nn.sigmoid(o_ref[:, sl].astype(F32))
        out_ref[:, sl] = ((hout * inv_rms * gain).T * gate).astype(BF16)

        m_new = mm[:, L - 1:L]
        decay = jnp.exp(m_old - m_new)
        kw = k.astype(F32) * jnp.exp(a_bc[:, :d] - m_new)
        ct_scr[hh] = decay * ct_scr[hh] + _dot(vt, kw.astype(BF16))
        n_scr[hh] = decay * n_scr[hh] + jnp.sum(kw, axis=0, keepdims=True)
        m_scr[hh] = jnp.broadcast_to(m_new, m_scr.shape[1:])


def _mlstm(z, vt, rows, ab, out_g, *, batch, seq, heads, d, vt_block):
    L = _fit(seq, TILES["mlstm_chunk"], LANES)
    assert L == d
    nc = seq // L
    width = heads * d
    n_rows = rows.shape[1]
    kern = functools.partial(_mlstm_kernel, heads=heads, d=d)
    og_b = jnp.broadcast_to(out_g.reshape(width, 1), (width, LANES))

    def zspec(group):
        return pl.BlockSpec((L, width), lambda b, c: (b * nc + c, group))

    return pl.pallas_call(
        kern,
        grid=(batch, nc),
        in_specs=[
            zspec(0), zspec(1), zspec(2),
            pl.BlockSpec((None, width, L), lambda b, c: (b, vt_block, c)),
            pl.BlockSpec((None, n_rows, L), lambda b, c: (b, 0, c)),
            pl.BlockSpec((L, LANES), lambda b, c: (b * nc + c, 0)),
            pl.BlockSpec((width, LANES), lambda b, c: (0, 0)),
        ],
        out_specs=pl.BlockSpec((L, width), lambda b, c: (b * nc + c, 0)),
        out_shape=jax.ShapeDtypeStruct((batch * seq, width), BF16),
        scratch_shapes=[pltpu.VMEM((heads, d, d), F32), pltpu.VMEM((heads, 1, d), F32), pltpu.VMEM((heads, 8, LANES), F32)],
        compiler_params=_params(("parallel", "arbitrary"), 32),
        name="mlstm",
    )(z, z, z, vt, rows, ab, og_b)


def _fox_kernel(fend_ref, fq_ref, win_ref, q_ref, k_ref, kb_ref, vt_ref, f_ref, o_ref, qa_scr, s0_scr, s1_scr, acc_scr, *, tq, tk, heads):
    b, h, i = pl.program_id(0), pl.program_id(1), pl.program_id(2)
    n_diag = tq // tk
    q0 = pl.multiple_of(i * tq, tq)
    lane = lax.broadcasted_iota(jnp.int32, (tq, LANES), 1)
    qa_scr[:, :LANES] = q_ref[...]
    qa_scr[:, LANES:] = jnp.where((lane >= N_BIAS_TERMS * h) & (lane < N_BIAS_TERMS * (h + 1)), 1.0, 0.0).astype(BF16)
    acc_scr[...] = jnp.zeros_like(acc_scr)

    def f_end(k0):
        return f_ref[:, pl.ds(pl.multiple_of(k0 + tk - LANES, LANES), LANES)][:, LANES - 1:LANES]

    f_ref_q = f_end(q0 + tq - tk)

    def key_block(k0):
        return jnp.concatenate([k_ref[pl.ds(k0, tk), :], kb_ref[pl.ds(k0, tk), :]], axis=1)

    def scores(k0, s_ref):
        s_ref[...] = _dot_nt(key_block(k0), qa_scr[...])

    def update(k0, s, carry, c0):
        m, l = carry
        cols = slice(c0, None) if c0 else slice(None)
        off = (f_ref_q - f_end(k0)) * LOG2E
        m_new = jnp.maximum(m[:, cols], jnp.max(s, axis=0, keepdims=True) + off)
        alpha = jnp.exp2(m[:, cols] - m_new)
        p = jnp.exp2(s - (m_new - off))
        l_new = alpha * l[:, cols] + jnp.sum(p, axis=0, keepdims=True)
        acc_scr[:, cols] = alpha * acc_scr[:, cols] + _dot(vt_ref[:, pl.ds(k0, tk)], p.astype(BF16))
        if c0:
            m_new = jnp.concatenate([m[:, :c0], m_new], axis=1)
            l_new = jnp.concatenate([l[:, :c0], l_new], axis=1)
        return m_new, l_new

    row = b * heads + h
    thr = fq_ref[row, i] + win_ref[0, 0]
    n_skip = lax.fori_loop(0, i * n_diag, lambda j, c: c + jnp.where(fend_ref[row, j] >= thr, 1, 0), jnp.int32(0))
    n_proc = i * n_diag - n_skip
    lead = n_proc % 2
    b_first = n_skip + lead

    def block_start(blk):
        return pl.multiple_of(blk * tk, tk)

    scores(block_start(b_first), s0_scr)

    def single(_, carry):
        scores(block_start(n_skip), s1_scr)
        return update(block_start(n_skip), s1_scr[...], carry, 0)

    def pair(blk, carry):
        k0 = block_start(blk)
        scores(k0 + tk, s1_scr)
        carry = update(k0, s0_scr[...], carry, 0)
        scores(k0 + 2 * tk, s0_scr)
        return update(k0 + tk, s1_scr[...], carry, 0)

    carry = (jnp.full((1, tq), NEG_BIG, F32), jnp.zeros((1, tq), F32))
    carry = lax.fori_loop(0, lead, single, carry)
    n_pairs = n_proc // 2
    carry = lax.fori_loop(0, n_pairs // 2, lambda u, c: pair(b_first + 4 * u + 2, pair(b_first + 4 * u, c)), carry)
    carry = lax.fori_loop(n_pairs // 2 * 2, n_pairs, lambda u, c: pair(b_first + 2 * u, c), carry)

    slots = (s0_scr, s1_scr)
    tri = lax.broadcasted_iota(jnp.int32, (tk, tk), 0) <= lax.broadcasted_iota(jnp.int32, (tk, tk), 1)
    for jj in range(n_diag):
        if jj + 1 < n_diag:
            c1 = (jj + 1) * tk
            slots[(jj + 1) % 2][:, c1:] = _dot_nt(key_block(q0 + c1), qa_scr[c1:, :])
        c0 = jj * tk
        s = slots[jj % 2][:, c0:]
        head = jnp.where(tri, s[:, :tk], NEG_BIG)
        s = head if s.shape[1] == tk else jnp.concatenate([head, s[:, tk:]], axis=1)
        carry = update(q0 + c0, s, carry, c0)
    _, l = carry
    o_ref[...] = (acc_scr[...] / l).T.astype(BF16)


def _fox(z, kb, vt, frow, win, *, batch, seq, heads, d, tq, tk, q_group, f_row0):
    assert (tq // tk) % 2 == 0
    nq = seq // tq
    per_group = z.shape[1] // (q_group + 2) // d
    kern = functools.partial(_fox_kernel, tq=tq, tk=tk, heads=heads)
    f4 = frow.reshape(batch, frow.shape[1], 1, seq)
    fh = frow[:, f_row0:f_row0 + heads, :]
    f_end = fh[:, :, tk - 1::tk].reshape(batch * heads, seq // tk)
    f_q = fh[:, :, ::tq].reshape(batch * heads, nq)
    smem = pl.BlockSpec(memory_space=pltpu.SMEM)
    return pl.pallas_call(
        kern,
        grid=(batch, heads, nq),
        in_specs=[
            smem, smem, smem,
            pl.BlockSpec((tq, d), lambda b, h, i: (b * nq + i, q_group * per_group + h)),
            pl.BlockSpec((seq, d), lambda b, h, i: (b, (q_group + 1) * per_group + h)),
            pl.BlockSpec((seq, LANES), lambda b, h, i: (b, 0)),
            pl.BlockSpec((None, d, seq), lambda b, h, i: (b, h, 0)),
            pl.BlockSpec((None, None, 1, seq), lambda b, h, i: (b, f_row0 + h, 0, 0)),
        ],
        out_specs=pl.BlockSpec((tq, d), lambda b, h, i: (b * nq + i, h)),
        out_shape=jax.ShapeDtypeStruct((batch * seq, heads * d), BF16),
        scratch_shapes=[pltpu.VMEM((tq, 2 * LANES), BF16), pltpu.VMEM((tk, tq), F32), pltpu.VMEM((tk, tq), F32), pltpu.VMEM((d, tq), F32)],
        compiler_params=_params(("parallel", "parallel", "arbitrary"), 48),
        name="fox_attention",
    )(f_end, f_q, win, z, z, kb, vt, f4)


def _out_proj_kernel(x_ref, ha_ref, hb_ref, wa_ref, wb_ref, g_ref, x1_ref, h_ref):
    x1 = x_ref[...] + _dot(ha_ref[...], wa_ref[...]) + _dot(hb_ref[...], wb_ref[...])
    x1_ref[...] = x1
    h_ref[...] = _rms(x1, g_ref[...]).astype(BF16)


def _out_proj(x2d, ha, hb, wa, wb, g):
    n, d = x2d.shape
    tm = _fit(n, TILES["out_tm"], BF16_SUBLANES)
    ka, kb = ha.shape[1], hb.shape[1]
    const = dict(pipeline_mode=pl.Buffered(1))
    return pl.pallas_call(
        _out_proj_kernel,
        grid=(n // tm,),
        in_specs=[
            pl.BlockSpec((tm, d), lambda i: (i, 0)),
            pl.BlockSpec((tm, ka), lambda i: (i, 0)),
            pl.BlockSpec((tm, kb), lambda i: (i, 0)),
            pl.BlockSpec((ka, d), lambda i: (0, 0), **const),
            pl.BlockSpec((kb, d), lambda i: (0, 0), **const),
            pl.BlockSpec((1, d), lambda i: (0, 0)),
        ],
        out_specs=[pl.BlockSpec((tm, d), lambda i: (i, 0)), pl.BlockSpec((tm, d), lambda i: (i, 0))],
        out_shape=[jax.ShapeDtypeStruct((n, d), F32), jax.ShapeDtypeStruct((n, d), BF16)],
        compiler_params=_params(("parallel",), 56),
        name="out_proj",
    )(x2d, ha, hb, wa, wb, g)


def _ffn_up_kernel(h_ref, halo_ref, wa_ref, wg_ref, cw_ref, cb_ref, act_ref, wa_scr, wg_scr, a_scr, *, tiles_per_seq, halo):
    i = pl.program_id(1)

    @pl.when(i == 0)
    def _():
        wa_scr[...] = wa_ref[...].astype(BF16)
        wg_scr[...] = wg_ref[...].astype(BF16)

    tm = h_ref.shape[0]
    h = h_ref[...]
    a = _dot(h, wa_scr[...])
    a_prev = _dot(halo_ref[...], wa_scr[...])
    a_prev = jnp.where(i % tiles_per_seq == 0, 0.0, a_prev)
    a_scr[0:halo, :] = a_prev
    a_scr[halo:, :] = a
    y = (cw_ref[2:3, :] * a + cw_ref[1:2, :] * a_scr[pl.ds(halo - 1, tm), :]
         + cw_ref[0:1, :] * a_scr[pl.ds(halo - 2, tm), :] + cb_ref[...])
    gelu = 0.5 * y * (1.0 + lax.erf(y * (0.5 ** 0.5)))
    act_ref[...] = (gelu * _dot(h, wg_scr[...])).astype(BF16)


def _ffn_up(h2, w_up, conv_w, conv_b, *, seq):
    n, d = h2.shape
    d_ff = conv_w.shape[1]
    halo = BF16_SUBLANES
    tm = _fit(seq, TILES["up_tm"], halo)
    tn = _fit(d_ff, TILES["up_tn"], LANES)
    nj = d_ff // tn
    kern = functools.partial(_ffn_up_kernel, tiles_per_seq=seq // tm, halo=halo)
    return pl.pallas_call(
        kern,
        grid=(nj, n // tm),
        in_specs=[
            pl.BlockSpec((tm, d), lambda j, i: (i, 0)),
            pl.BlockSpec((halo, d), lambda j, i: (jnp.maximum(i * (tm // halo) - 1, 0), 0)),
            pl.BlockSpec((d, tn), lambda j, i: (0, j)),
            pl.BlockSpec((d, tn), lambda j, i: (0, nj + j)),
            pl.BlockSpec((conv_w.shape[0], tn), lambda j, i: (0, j)),
            pl.BlockSpec((1, tn), lambda j, i: (0, j)),
        ],
        out_specs=pl.BlockSpec((tm, tn), lambda j, i: (i, j)),
        out_shape=jax.ShapeDtypeStruct((n, d_ff), BF16),
        scratch_shapes=[pltpu.VMEM((d, tn), BF16), pltpu.VMEM((d, tn), BF16), pltpu.VMEM((tm + halo, tn), F32)],
        compiler_params=_params(("arbitrary", "arbitrary"), 56),
        name="ffn_up",
    )(h2, h2, w_up, w_up, conv_w, conv_b)


def _ffn_down_kernel(act_ref, w_ref, x_ref, o_ref):
    o_ref[...] = x_ref[...] + _dot(act_ref[...], w_ref[...])


def _ffn_down(act, w_down, x1):
    n, d_ff = act.shape
    d = w_down.shape[1]
    tm = _fit(n, TILES["down_tm"], BF16_SUBLANES)
    tn = _fit(d, TILES["down_tn"], LANES)
    return pl.pallas_call(
        _ffn_down_kernel,
        grid=(d // tn, n // tm),
        in_specs=[
            pl.BlockSpec((tm, d_ff), lambda j, i: (i, 0)),
            pl.BlockSpec((d_ff, tn), lambda j, i: (0, j)),
            pl.BlockSpec((tm, tn), lambda j, i: (i, j)),
        ],
        out_specs=pl.BlockSpec((tm, tn), lambda j, i: (i, j)),
        out_shape=jax.ShapeDtypeStruct((n, d), F32),
        compiler_params=_params(("arbitrary", "arbitrary"), 56),
        name="ffn_down",
    )(act, w_down, x1)


def _ple_kernel(x_ref, p_ref, g_ref, wg_ref, wp_ref, o_ref, wg_scr, wp_scr):
    @pl.when(pl.program_id(0) == 0)
    def _():
        wg_scr[...] = wg_ref[...].astype(BF16)
        wp_scr[...] = wp_ref[...].astype(BF16)

    x = x_ref[...]
    gate = jax.nn.sigmoid(_dot(_rms(x, g_ref[...]).astype(BF16), wg_scr[...]))
    o_ref[...] = x + gate * _dot(p_ref[...].astype(BF16), wp_scr[...])


def _ple(x2, p2d, g, w_gate, w_proj):
    n, d = x2.shape
    dp = p2d.shape[1]
    tm = _fit(n, TILES["ple_tm"], 8)
    const = dict(pipeline_mode=pl.Buffered(1))
    return pl.pallas_call(
        _ple_kernel,
        grid=(n // tm,),
        in_specs=[
            pl.BlockSpec((tm, d), lambda i: (i, 0)),
            pl.BlockSpec((tm, dp), lambda i: (i, 0)),
            pl.BlockSpec((1, d), lambda i: (0, 0)),
            pl.BlockSpec((d, d), lambda i: (0, 0), **const),
            pl.BlockSpec((dp, d), lambda i: (0, 0), **const),
        ],
        out_specs=pl.BlockSpec((tm, d), lambda i: (i, 0)),
        out_shape=jax.ShapeDtypeStruct((n, d), F32),
        scratch_shapes=[pltpu.VMEM((d, d), BF16), pltpu.VMEM((dp, d), BF16)],
        compiler_params=_params(("arbitrary",), 58),
        name="ple",
    )(x2, p2d, g, w_gate, w_proj)


def kernel(x, p, mix_norm_g, w_in, mlstm_gate_bias, mlstm_out_g, fox_q_g, fox_k_g, fox_f_bias, w_out, ffn_norm_g, w_up, conv_w, conv_b, w_down, ple_norm_g, w_ple_gate, w_ple_proj):
    batch, seq, d_model = x.shape
    depth = w_in.shape[0]
    mh = mlstm_gate_bias.shape[1] // 2
    mw = mlstm_out_g.shape[1]
    md = mw // mh
    fh = fox_f_bias.shape[1]
    fd = fox_q_g.shape[1]
    fw = fh * fd
    n_gate = 2 * mh + fh
    n_rows = -(-n_gate // 8) * 8
    assert mw == fw and fd == LANES and n_rows <= LANES and N_BIAS_TERMS * fh <= LANES
    assert w_in.shape[2] == 4 * mw + 2 * mh + 3 * fw + fh
    tq = _fit(seq, TILES["fox_tq"], LANES)
    tk = _fit(tq, TILES["fox_tk"], LANES)

    xs = x.reshape(batch * seq, d_model)
    for li in range(depth):
        w_plain, w_qk, w_v, w_gate = _regroup(w_in[li].T, mw=mw, fw=fw, mh=mh, fh=fh, k_scale=md ** -0.5, n_rows=n_rows)
        pad = jnp.zeros((d_model, n_rows - n_gate), F32)
        gate_bias = jnp.concatenate([mlstm_gate_bias[li], pad[0], fox_f_bias[li], jnp.zeros((LANES - n_rows,), F32)])[None, :]
        q_gain = (fox_q_g[li] * (fd ** -0.5 * LOG2E))[None, :]
        k_gain = fox_k_g[li][None, :]

        zm, zg, h1 = _proj_plain(xs, mix_norm_g[li][None, :], w_plain, w_gate, tn=mw)
        zf = _proj_headnorm(h1, w_qk, jnp.stack([q_gain, k_gain]), tn=fw)
        vt = _proj_transposed(h1, w_v, batch=batch)
        rows, kb, ab = _gates(zg, gate_bias, batch=batch, seq=seq, blk=tk, n_pass=mh, n_rows=n_rows, n_attn=fh)
        ha = _mlstm(zm, vt, rows, ab, mlstm_out_g[li], batch=batch, seq=seq, heads=mh, d=md, vt_block=1)
        qk_bound = 1.02 * fd * jnp.max(jnp.abs(q_gain)) * jnp.max(jnp.abs(k_gain))
        win = ((2.0 * qk_bound + F32_MIN_EXP) / LOG2E).reshape(1, 1)
        hb = _fox(zf, kb, vt, rows, win, batch=batch, seq=seq, heads=fh, d=fd, tq=tq, tk=tk, q_group=0, f_row0=n_rows - fh)

        wo = w_out[li].astype(BF16)
        x1, h2 = _out_proj(xs, ha, hb, wo[:mw], wo[mw:], ffn_norm_g[li][None, :])
        act = _ffn_up(h2, w_up[li], conv_w[li], conv_b[li][None, :], seq=seq)
        x2 = _ffn_down(act, w_down[li].astype(BF16), x1)
        xs = _ple(x2, p[li].reshape(batch * seq, -1), ple_norm_g[li][None, :], w_ple_gate[li], w_ple_proj[li])
    return xs.reshape(batch, seq, d_model)
```
